```python
import jax, jax.numpy as jnp
from jax import lax
import numpy as np

D_MODEL = 2048
BATCH = 4
SEQ = 2048
DEPTH = 1

CHUNK = 64
HEAD_DIM = 128
N_MIX_HEADS = D_MODEL // HEAD_DIM
SB_HEADS = N_MIX_HEADS // 2
RET_HEADS = N_MIX_HEADS - SB_HEADS
SB_WIDTH = SB_HEADS * HEAD_DIM
RET_WIDTH = RET_HEADS * HEAD_DIM
MIX_WIDTH = SB_WIDTH + RET_WIDTH
IN_WIDTH = 3 * SB_WIDTH + 4 * RET_WIDTH
Q_BLOCK = 128
ROPE_THETA = 10000.0
PEER_HEADS = 8
PEER_KEYS = 128
PEER_EXPERTS = PEER_KEYS * PEER_KEYS
PEER_KEY_DIM = 256
PEER_HALF = PEER_KEY_DIM // 2
PEER_TOPK = 16
PEER_TOKEN_BLOCK = 128
LN_EPS = 1e-5
NORM_EPS = 1e-6
DEEPNORM_ALPHA = (2.0 * DEPTH) ** 0.25
DEEPNORM_BETA = (8.0 * DEPTH) ** -0.25

kernel_name = 'hybrid_sb_retention_peer_block'


def layer_norm(x, w, b):
    xf = x.astype(jnp.float32)
    mu = jnp.mean(xf, -1, keepdims=True)
    var = jnp.mean(jnp.square(xf - mu), -1, keepdims=True)
    y = (xf - mu) * lax.rsqrt(var + LN_EPS) * w.astype(jnp.float32) + b.astype(jnp.float32)
    return y.astype(x.dtype)


def to_heads(t, n_heads):
    b, s, _ = t.shape
    return t.reshape(b, s, n_heads, HEAD_DIM).transpose(0, 2, 1, 3).astype(jnp.float32)


def from_heads(t):
    b, h, s, d = t.shape
    return t.transpose(0, 2, 1, 3).reshape(b, s, h * d)


def rotary(t):
    s, d = t.shape[2], t.shape[3]
    inv_freq = ROPE_THETA ** (-jnp.arange(0, d, 2, dtype=jnp.float32) / d)
    ang = jnp.arange(s, dtype=jnp.float32)[:, None] * inv_freq[None, :]
    ang = jnp.concatenate([ang, ang], -1)
    t1, t2 = jnp.split(t, 2, axis=-1)
    return t * jnp.cos(ang) + jnp.concatenate([-t2, t1], -1) * jnp.sin(ang)


def stick_breaking_attention(q, k, v):
    b, h, s, d = q.shape
    scale = d ** -0.5
    outs = []
    for blk in range(s // Q_BLOCK):
        q0 = blk * Q_BLOCK
        kv_len = q0 + Q_BLOCK
        z = jnp.einsum('bhqd,bhkd->bhqk', q[:, :, q0:kv_len], k[:, :, :kv_len]) * scale
        q_pos = q0 + jnp.arange(Q_BLOCK)[:, None]
        k_pos = jnp.arange(kv_len)[None, :]
        mask = k_pos < q_pos
        log_1mb = jnp.where(mask, jax.nn.log_sigmoid(-z), 0.0)
        incl = jnp.cumsum(log_1mb, axis=-1)
        log_a = jax.nn.log_sigmoid(z) + incl[..., -1:] - incl
        a = jnp.where(mask, jnp.exp(log_a), 0.0)
        outs.append(jnp.einsum('bhqk,bhkd->bhqd', a, v[:, :, :kv_len]))
    return jnp.concatenate(outs, axis=2)


def retention(q, k, v):
    b, h, s, d = q.shape
    nc = s // CHUNK
    log_g = jnp.log1p(-(2.0 ** (-5.0 - jnp.arange(h, dtype=jnp.float32))))
    idx = jnp.arange(CHUNK, dtype=jnp.float32)
    intra_decay = jnp.exp(log_g[:, None, None] * jnp.abs(idx[:, None] - idx[None, :]))
    xi = jnp.exp(log_g[:, None] * (idx + 1.0))
    zeta = jnp.exp(log_g[:, None] * (CHUNK - 1.0 - idx))
    chunk_decay = jnp.exp(log_g * CHUNK)
    qc = q.reshape(b, h, nc, CHUNK, d)
    kc = (k * d ** -0.5).reshape(b, h, nc, CHUNK, d)
    vc = v.reshape(b, h, nc, CHUNK, v.shape[-1])
    scores = jnp.einsum('bhnid,bhnjd->bhnij', qc, kc) * intra_decay[None, :, None]
    intra = jnp.einsum('bhnij,bhnjv->bhniv', scores, vc)
    kv = jnp.einsum('bhnjd,bhnjv->nbhdv', kc * zeta[None, :, None, :, None], vc)

    def step(state, kv_n):
        return state * chunk_decay[None, :, None, None] + kv_n, state

    _, prev = lax.scan(step, jnp.zeros(kv.shape[1:], kv.dtype), kv)
    inter = jnp.einsum('bhnid,nbhdv->bhniv', qc, prev) * xi[None, :, None, :, None]
    return (intra + inter).reshape(b, h, s, -1)


def token_mixer(x, w_in, sb_norm_w, ret_norm_w, w_out):
    b, s, _ = x.shape
    proj = x @ w_in
    cuts = [SB_WIDTH, 2 * SB_WIDTH, 3 * SB_WIDTH, 3 * SB_WIDTH + RET_WIDTH,
            3 * SB_WIDTH + 2 * RET_WIDTH, 3 * SB_WIDTH + 3 * RET_WIDTH]
    sb_q, sb_k, sb_v, r_q, r_k, r_v, r_g = jnp.split(proj, cuts, axis=-1)
    sb = stick_breaking_attention(to_heads(sb_q, SB_HEADS), to_heads(sb_k, SB_HEADS), to_heads(sb_v, SB_HEADS))
    sb = sb * lax.rsqrt(jnp.mean(jnp.square(sb), -1, keepdims=True) + NORM_EPS)
    sb = from_heads(sb) * sb_norm_w.astype(jnp.float32)
    ret = retention(rotary(to_heads(r_q, RET_HEADS)), rotary(to_heads(r_k, RET_HEADS)), to_heads(r_v, RET_HEADS))
    mu = jnp.mean(ret, -1, keepdims=True)
    var = jnp.mean(jnp.square(ret - mu), -1, keepdims=True)
    ret = from_heads((ret - mu) * lax.rsqrt(var + NORM_EPS)) * ret_norm_w.astype(jnp.float32)
    ret = jax.nn.silu(r_g.astype(jnp.float32)) * ret
    mixed = jnp.concatenate([sb, ret], axis=-1).astype(x.dtype)
    return mixed @ w_out


def peer_ffn(x, w_pq, sub_keys, expert_u, expert_v):
    b, s, d = x.shape
    n_tok = b * s
    xt = x.reshape(n_tok, d)
    q = (xt @ w_pq).reshape(n_tok, PEER_HEADS, 2, PEER_HALF).astype(jnp.float32)
    scores = jnp.einsum('thcd,hckd->thck', q, sub_keys.astype(jnp.float32))
    s_top, i_top = lax.top_k(scores, PEER_TOPK)
    cand = (s_top[:, :, 0, :, None] + s_top[:, :, 1, None, :]).reshape(n_tok, PEER_HEADS, PEER_TOPK * PEER_TOPK)
    cand_idx = (i_top[:, :, 0, :, None] * PEER_KEYS + i_top[:, :, 1, None, :]).reshape(n_tok, PEER_HEADS, PEER_TOPK * PEER_TOPK)
    best, pos = lax.top_k(cand, PEER_TOPK)
    expert_idx = jnp.take_along_axis(cand_idx, pos, axis=-1)
    gate = jax.nn.softmax(best, axis=-1)
    nb = n_tok // PEER_TOKEN_BLOCK

    def block(args):
        xb, idx_b, g_b = args
        u = jnp.take(expert_u, idx_b, axis=0)
        act = jax.nn.gelu(jnp.einsum('thkd,td->thk', u, xb).astype(jnp.float32), approximate=False)
        v = jnp.take(expert_v, idx_b, axis=0)
        return jnp.einsum('thk,thkd->td', (g_b * act).astype(v.dtype), v)

    out = lax.map(block, (xt.reshape(nb, PEER_TOKEN_BLOCK, d),
                          expert_idx.reshape(nb, PEER_TOKEN_BLOCK, PEER_HEADS, PEER_TOPK),
                          gate.reshape(nb, PEER_TOKEN_BLOCK, PEER_HEADS, PEER_TOPK)))
    return out.reshape(b, s, d).astype(x.dtype)


def setup_inputs(seed: int = 0) -> dict:
    key = jax.random.key(seed)
    ks = jax.random.split(key, 14)
    f32 = jnp.float32

    def nrm(k, shape, scale):
        return scale * jax.random.normal(k, shape, f32)

    return {
        'x': jax.random.normal(ks[0], (BATCH, SEQ, D_MODEL), f32),
        'w_in': nrm(ks[1], (DEPTH, D_MODEL, IN_WIDTH), D_MODEL ** -0.5),
        'sb_norm_w': 1.0 + nrm(ks[2], (DEPTH, SB_WIDTH), 0.02),
        'ret_norm_w': 1.0 + nrm(ks[3], (DEPTH, RET_WIDTH), 0.02),
        'w_out': nrm(ks[4], (DEPTH, MIX_WIDTH, D_MODEL), DEEPNORM_BETA * MIX_WIDTH ** -0.5),
        'ln1_w': 1.0 + nrm(ks[5], (DEPTH, D_MODEL), 0.02),
        'ln1_b': nrm(ks[6], (DEPTH, D_MODEL), 0.02),
        'w_pq': nrm(ks[7], (DEPTH, D_MODEL, PEER_HEADS * PEER_KEY_DIM), D_MODEL ** -0.5),
        'sub_keys': nrm(ks[8], (DEPTH, PEER_HEADS, 2, PEER_KEYS, PEER_HALF), PEER_HALF ** -0.5),
        'expert_u': nrm(ks[9], (DEPTH, PEER_EXPERTS, D_MODEL), D_MODEL ** -0.5),
        'expert_v': nrm(ks[10], (DEPTH, PEER_EXPERTS, D_MODEL), DEEPNORM_BETA),
        'ln2_w': 1.0 + nrm(ks[11], (DEPTH, D_MODEL), 0.02),
        'ln2_b': nrm(ks[12], (DEPTH, D_MODEL), 0.02),
    }


def reference(x, w_in, sb_norm_w, ret_norm_w, w_out, ln1_w, ln1_b, w_pq, sub_keys,
              expert_u, expert_v, ln2_w, ln2_b):
    for layer in range(DEPTH):
        mix = token_mixer(x, w_in[layer], sb_norm_w[layer], ret_norm_w[layer], w_out[layer])
        x = layer_norm(DEEPNORM_ALPHA * x + mix, ln1_w[layer], ln1_b[layer])
        ffn = peer_ffn(x, w_pq[layer], sub_keys[layer], expert_u[layer], expert_v[layer])
        x = layer_norm(DEEPNORM_ALPHA * x + ffn, ln2_w[layer], ln2_b[layer])
    return x
```

```python
import functools
import math

import jax
import jax.numpy as jnp
from jax import lax
from jax.experimental import pallas as pl
from jax.experimental.pallas import tpu as pltpu

F32 = jnp.float32
BF16 = jnp.bfloat16

LANES = 128
HEAD_DIM = 128
CHUNK = 64
ROPE_THETA = 10000.0
PEER_HEADS = 8
PEER_KEYS = 128
PEER_TOPK = 16
LN_EPS = 1e-5
NORM_EPS = 1e-6

V7X_VMEM_BYTES = 64 * 1024 * 1024
VMEM_LIMIT = 52 * 1024 * 1024

NT_DIMS = (((1,), (1,)), ((), ()))
TN_DIMS = (((0,), (0,)), ((), ()))


def _params(semantics):
    return pltpu.CompilerParams(dimension_semantics=semantics, vmem_limit_bytes=VMEM_LIMIT)


def _matmul_kernel(x_ref, w_ref, o_ref):
    o_ref[...] = jnp.dot(x_ref[...], w_ref[...], preferred_element_type=F32).astype(o_ref.dtype)


def _matmul(x, w, bm, bn, out_dtype):
    m, k = x.shape
    n = w.shape[1]
    return pl.pallas_call(
        _matmul_kernel,
        grid=(n // bn, m // bm),
        in_specs=[pl.BlockSpec((bm, k), lambda j, i: (i, 0)),
                  pl.BlockSpec((k, bn), lambda j, i: (0, j))],
        out_specs=pl.BlockSpec((bm, bn), lambda j, i: (i, j)),
        out_shape=jax.ShapeDtypeStruct((m, n), out_dtype),
        compiler_params=_params(("parallel", "parallel")),
        name="dense_proj",
    )(x, w)


def _sb_kernel(q_ref, k_ref, v_ref, tri_ref, w_ref, o_ref, *, tq, tk, scale):
    qi = pl.program_id(2)
    q = q_ref[0]
    tri = tri_ref[...]
    row = lax.broadcasted_iota(jnp.int32, (tq, tk), 0)
    col = lax.broadcasted_iota(jnp.int32, (tq, tk), 1)

    def body(jj, carry):
        acc, run = carry
        j = qi - jj
        start = pl.multiple_of(j * tk, tk)
        ks = k_ref[0, pl.ds(start, tk), :]
        vs = v_ref[0, pl.ds(start, tk), :]
        z = lax.dot_general(q, ks, NT_DIMS, preferred_element_type=F32) * scale
        softplus = jnp.maximum(z, 0.0) + jnp.log1p(jnp.exp(-jnp.abs(z)))
        mask = (col + j * tk) < (row + qi * tq)
        log_1mb = jnp.where(mask, -softplus, 0.0)
        hi = log_1mb.astype(BF16)
        lo = (log_1mb - hi.astype(F32)).astype(BF16)
        sums = (jnp.dot(hi, tri, preferred_element_type=F32)
                + jnp.dot(lo, tri, preferred_element_type=F32))
        log_a = (z - softplus) + sums[:, :tk] + run
        a = jnp.where(mask, jnp.exp(log_a), 0.0)
        acc = acc + jnp.dot(a.astype(BF16), vs, preferred_element_type=F32)
        return acc, run + sums[:, tk:]

    acc, _ = lax.fori_loop(0, qi + 1, body,
                           (jnp.zeros((tq, HEAD_DIM), F32), jnp.zeros((tq, tk), F32)))
    ms = jnp.mean(acc * acc, axis=-1, keepdims=True)
    o_ref[0] = (acc * lax.rsqrt(ms + NORM_EPS) * w_ref[...]).astype(o_ref.dtype)


def _sb_attention(proj, norm_w, n_heads, col0):
    b, s, _ = proj.shape
    tq = tk = 128
    tri = jnp.concatenate(
        [(jnp.arange(tk)[:, None] > jnp.arange(tk)[None, :]).astype(BF16), jnp.ones((tk, tk), BF16)], axis=1)
    kern = functools.partial(_sb_kernel, tq=tq, tk=tk, scale=HEAD_DIM ** -0.5)
    return pl.pallas_call(
        kern,
        grid=(b, n_heads, s // tq),
        in_specs=[pl.BlockSpec((1, tq, HEAD_DIM), lambda bi, h, i: (bi, i, col0 + h)),
                  pl.BlockSpec((1, s, HEAD_DIM), lambda bi, h, i: (bi, 0, col0 + n_heads + h)),
                  pl.BlockSpec((1, s, HEAD_DIM), lambda bi, h, i: (bi, 0, col0 + 2 * n_heads + h)),
                  pl.BlockSpec((tk, 2 * tk), lambda bi, h, i: (0, 0)),
                  pl.BlockSpec((1, HEAD_DIM), lambda bi, h, i: (0, h))],
        out_specs=pl.BlockSpec((1, tq, HEAD_DIM), lambda bi, h, i: (bi, i, h)),
        out_shape=jax.ShapeDtypeStruct((b, s, n_heads * HEAD_DIM), BF16),
        compiler_params=_params(("parallel", "parallel", "arbitrary")),
        name="sb_attention",
    )(proj, proj, proj, tri, norm_w.reshape(1, -1).astype(F32))


def _ret_kernel(q_ref, k_ref, v_ref, g_ref, cos_ref, sin_ref, dec_ref, xi_ref, zeta_ref, gt_ref, w_ref,
                o_ref, state_ref):
    @pl.when(pl.program_id(2) == 0)
    def _():
        state_ref[...] = jnp.zeros_like(state_ref)

    cos = cos_ref[...]
    sin = sin_ref[...]
    q = q_ref[0].astype(F32)
    k = k_ref[0].astype(F32)
    half = HEAD_DIM // 2
    qr = q * cos + pltpu.roll(q, half, 1) * sin
    kr = k * cos + pltpu.roll(k, half, 1) * sin
    qb = qr.astype(BF16)
    v = v_ref[0]
    state = state_ref[...]
    scores = lax.dot_general(qb, kr.astype(BF16), NT_DIMS, preferred_element_type=F32) * dec_ref[0]
    ret = jnp.dot(scores.astype(BF16), v, preferred_element_type=F32)
    ret = ret + jnp.dot(qb, state.astype(BF16), preferred_element_type=F32) * xi_ref[0]
    kz = (kr * zeta_ref[0]).astype(BF16)
    state_ref[...] = state * gt_ref[0] + lax.dot_general(kz, v, TN_DIMS, preferred_element_type=F32)
    mu = jnp.mean(ret, axis=-1, keepdims=True)
    cen = ret - mu
    var = jnp.mean(cen * cen, axis=-1, keepdims=True)
    g = g_ref[0].astype(F32)
    silu = g / (1.0 + jnp.exp(-g))
    o_ref[0] = (cen * lax.rsqrt(var + NORM_EPS) * w_ref[...] * silu).astype(o_ref.dtype)


def _retention(proj, norm_w, n_heads, col0):
    b, s, _ = proj.shape
    t = 256
    d = HEAD_DIM
    inv_freq = ROPE_THETA ** (-jnp.arange(0, d, 2, dtype=F32) / d)
    ang = jnp.arange(s, dtype=F32)[:, None] * inv_freq[None, :]
    ang = jnp.concatenate([ang, ang], -1)
    sign = jnp.concatenate([-jnp.ones((d // 2,), F32), jnp.ones((d // 2,), F32)])
    cos_t = jnp.cos(ang)
    sin_t = jnp.sin(ang) * sign[None, :]
    log_g = jnp.log1p(-(2.0 ** (-5.0 - jnp.arange(n_heads, dtype=F32))))
    idx = jnp.arange(t, dtype=F32)
    same_or_earlier_chunk = (jnp.floor(idx[None, :] / CHUNK) <= jnp.floor(idx[:, None] / CHUNK))
    dec = jnp.exp(log_g[:, None, None] * jnp.abs(idx[:, None] - idx[None, :])) * (d ** -0.5)
    dec = jnp.where(same_or_earlier_chunk[None], dec, 0.0)
    xi = jnp.broadcast_to(jnp.exp(log_g[:, None] * (idx + 1.0))[:, :, None], (n_heads, t, d))
    zeta = jnp.broadcast_to((jnp.exp(log_g[:, None] * (t - 1.0 - idx)) * (d ** -0.5))[:, :, None], (n_heads, t, d))
    gt = jnp.broadcast_to(jnp.exp(log_g * t)[:, None, None], (n_heads, 1, d))
    head_spec = lambda off: pl.BlockSpec((1, t, d), lambda bi, h, i: (bi, i, col0 + off * n_heads + h))
    return pl.pallas_call(
        _ret_kernel,
        grid=(b, n_heads, s // t),
        in_specs=[head_spec(0), head_spec(1), head_spec(2), head_spec(3),
                  pl.BlockSpec((t, d), lambda bi, h, i: (i, 0)),
                  pl.BlockSpec((t, d), lambda bi, h, i: (i, 0)),
                  pl.BlockSpec((1, t, t), lambda bi, h, i: (h, 0, 0)),
                  pl.BlockSpec((1, t, d), lambda bi, h, i: (h, 0, 0)),
                  pl.BlockSpec((1, t, d), lambda bi, h, i: (h, 0, 0)),
                  pl.BlockSpec((1, 1, d), lambda bi, h, i: (h, 0, 0)),
                  pl.BlockSpec((1, d), lambda bi, h, i: (0, h))],
        out_specs=pl.BlockSpec((1, t, d), lambda bi, h, i: (bi, i, h)),
        out_shape=jax.ShapeDtypeStruct((b, s, n_heads * d), BF16),
        scratch_shapes=[pltpu.VMEM((d, d), F32)],
        compiler_params=_params(("parallel", "parallel", "arbitrary")),
        name="retention",
    )(proj, proj, proj, proj, cos_t, sin_t, dec, xi, zeta, gt, norm_w.reshape(1, -1).astype(F32))


def _layer_norm(y, w, b):
    mu = jnp.mean(y, axis=-1, keepdims=True)
    cen = y - mu
    var = jnp.mean(cen * cen, axis=-1, keepdims=True)
    return cen * lax.rsqrt(var + LN_EPS) * w + b


def _outproj_ln_kernel(sb_ref, ret_ref, wo_ref, x_ref, lw_ref, lb_ref, o_ref, ob_ref, *, alpha, split):
    mix = jnp.dot(sb_ref[...], wo_ref[:split, :], preferred_element_type=F32)
    mix = mix + jnp.dot(ret_ref[...], wo_ref[split:, :], preferred_element_type=F32)
    y = _layer_norm(alpha * x_ref[...] + mix, lw_ref[...], lb_ref[...])
    o_ref[...] = y
    ob_ref[...] = y.astype(BF16)


def _outproj_ln(sb, ret, wo, x, lw, lb, alpha):
    m, d = x.shape
    split = sb.shape[1]
    bm = 256
    kern = functools.partial(_outproj_ln_kernel, alpha=alpha, split=split)
    row = lambda width: pl.BlockSpec((bm, width), lambda i: (i, 0))
    full = lambda shape: pl.BlockSpec(shape, lambda i: (0, 0))
    return pl.pallas_call(
        kern,
        grid=(m // bm,),
        in_specs=[row(split), row(ret.shape[1]), full(wo.shape), row(d), full((1, d)), full((1, d))],
        out_specs=[row(d), row(d)],
        out_shape=[jax.ShapeDtypeStruct((m, d), F32), jax.ShapeDtypeStruct((m, d), BF16)],
        compiler_params=_params(("parallel",)),
        name="outproj_ln",
    )(sb, ret, wo, x, lw.reshape(1, d), lb.reshape(1, d))


def _top_ranks(s, iota, k):
    n = s.shape[0]
    work = s
    rank = jnp.full(s.shape, float(k), F32)
    tops = []
    for r in range(k):
        m = jnp.max(work, axis=0, keepdims=True)
        first = jnp.min(jnp.where(work == m, iota, float(n)), axis=0, keepdims=True)
        sel = iota == first
        rank = jnp.where(sel, float(r), rank)
        work = jnp.where(sel, -jnp.inf, work)
        tops.append(m)
    return rank, tops


def _router_kernel(q_ref, keys_ref, cnt_ref, e1_ref, rank2_ref, e2_ref, *, lanes):
    nk = PEER_KEYS
    k = PEER_TOPK
    iota = lax.broadcasted_iota(jnp.int32, (nk, lanes), 0).astype(F32)
    widths = [k // (a + 1) for a in range(k)]
    n_cand = sum(widths)
    n_pad = -(-n_cand // 8) * 8
    ciota = lax.broadcasted_iota(jnp.int32, (n_pad, lanes), 0).astype(F32)

    def column(c, _):
        sl = pl.ds(pl.multiple_of(c * lanes, lanes), lanes)
        q = q_ref[sl, :]
        half = q.shape[1] // 2
        s1 = lax.dot_general(keys_ref[0, 0], q[:, :half], NT_DIMS, preferred_element_type=F32)
        s2 = lax.dot_general(keys_ref[0, 1], q[:, half:], NT_DIMS, preferred_element_type=F32)
        rank1, top1 = _top_ranks(s1, iota, k)
        rank2, top2 = _top_ranks(s2, iota, k)
        top2_all = jnp.concatenate(top2, axis=0)
        cand = [top1[a] + top2_all[:widths[a]] for a in range(k)]
        if n_pad > n_cand:
            cand.append(jnp.full((n_pad - n_cand, lanes), -jnp.inf, F32))
        cand = jnp.concatenate(cand, axis=0)
        work = cand
        chosen = jnp.zeros(cand.shape, F32)
        for _ in range(k):
            m = jnp.max(work, axis=0, keepdims=True)
            first = jnp.min(jnp.where(work == m, ciota, float(n_pad)), axis=0, keepdims=True)
            sel = ciota == first
            chosen = jnp.where(sel, 1.0, chosen)
            work = jnp.where(sel, -jnp.inf, work)
        best = top1[0] + top2[0]
        z = jnp.sum(jnp.where(chosen > 0.0, jnp.exp(cand - best), 0.0), axis=0, keepdims=True)
        cnt = jnp.zeros((nk, lanes), F32)
        off = 0
        for a in range(k):
            cnt_a = jnp.sum(chosen[off:off + widths[a]], axis=0, keepdims=True)
            cnt = jnp.where(rank1 == float(a), cnt_a, cnt)
            off += widths[a]
        cnt_ref[0, c] = cnt
        e1_ref[0, c] = jnp.exp(s1 - top1[0])
        rank2_ref[0, c] = rank2.astype(BF16)
        e2_ref[0, c] = (jnp.exp(s2 - top2[0]) / z).astype(BF16)
        return 0

    lax.fori_loop(0, q_ref.shape[0] // lanes, column, 0)


def _router(qp, sub_keys, tb):
    t = qp.shape[0]
    h, _, nk, half = sub_keys.shape
    out_f32 = jax.ShapeDtypeStruct((h, t // LANES, nk, LANES), F32)
    out_bf16 = jax.ShapeDtypeStruct((h, t // LANES, nk, LANES), BF16)
    out_spec = pl.BlockSpec((1, tb // LANES, nk, LANES), lambda j, hh: (hh, j, 0, 0))
    return pl.pallas_call(
        functools.partial(_router_kernel, lanes=LANES),
        grid=(t // tb, h),
        in_specs=[pl.BlockSpec((tb, 2 * half), lambda j, hh: (j, hh)),
                  pl.BlockSpec((1, 2, nk, half), lambda j, hh: (hh, 0, 0, 0))],
        out_specs=[out_spec, out_spec, out_spec, out_spec],
        out_shape=[out_f32, out_f32, out_bf16, out_bf16],
        compiler_params=_params(("parallel", "parallel")),
        name="peer_router",
    )(qp, sub_keys)


def _peer_kernel(x_ref, u_ref, v_ref, cnt_ref, e1_ref, rank2_ref, e2_ref, o_ref, acc_ref, act_ref, hid_ref,
                 *, n_sub):
    e = pl.program_id(1)

    @pl.when(e == 0)
    def _():
        acc_ref[...] = jnp.zeros_like(acc_ref)

    tb = x_ref.shape[0]
    nk = PEER_KEYS
    act_ref[...] = lax.dot_general(u_ref[...], x_ref[...], NT_DIMS, preferred_element_type=F32)
    for ii in range(n_sub):
        i1 = e * n_sub + ii
        rows = slice(ii * nk, (ii + 1) * nk)
        for c in range(tb // LANES):
            cols = slice(c * LANES, (c + 1) * LANES)
            gate = jnp.zeros((nk, LANES), BF16)
            for h in range(PEER_HEADS):
                cnt = jnp.broadcast_to(cnt_ref[h, c, pl.ds(i1, 1), :].astype(BF16), (nk, LANES))
                e1 = jnp.broadcast_to(e1_ref[h, c, pl.ds(i1, 1), :].astype(BF16), (nk, LANES))
                gate = gate + jnp.where(rank2_ref[h, c] < cnt, e2_ref[h, c] * e1, jnp.zeros((), BF16))
            a = act_ref[rows, cols]
            gelu = 0.5 * a * (1.0 + lax.erf(a * (2.0 ** -0.5)))
            hid_ref[rows, cols] = gelu.astype(BF16) * gate
    acc_ref[...] += lax.dot_general(hid_ref[...], v_ref[...], TN_DIMS, preferred_element_type=F32)

    @pl.when(e == pl.num_programs(1) - 1)
    def _():
        o_ref[...] = acc_ref[...]


def _peer_experts(xb, u, v, cnt, e1, rank2, e2, tb, eb):
    t, d = xb.shape
    n_exp = u.shape[0]
    h, _, nk, _ = cnt.shape
    route = pl.BlockSpec((h, tb // LANES, nk, LANES), lambda j, e: (0, j, 0, 0))
    return pl.pallas_call(
        functools.partial(_peer_kernel, n_sub=eb // nk),
        grid=(t // tb, n_exp // eb),
        in_specs=[pl.BlockSpec((tb, d), lambda j, e: (j, 0)),
                  pl.BlockSpec((eb, d), lambda j, e: (e, 0)),
                  pl.BlockSpec((eb, d), lambda j, e: (e, 0)),
                  route, route, route, route],
        out_specs=pl.BlockSpec((tb, d), lambda j, e: (j, 0)),
        out_shape=jax.ShapeDtypeStruct((t, d), F32),
        scratch_shapes=[pltpu.VMEM((tb, d), F32), pltpu.VMEM((eb, tb), F32), pltpu.VMEM((eb, tb), BF16)],
        compiler_params=_params(("parallel", "arbitrary")),
        name="peer_experts",
    )(xb, u, v, cnt, e1, rank2, e2)


def _res_ln_kernel(x_ref, f_ref, lw_ref, lb_ref, o_ref, *, alpha):
    o_ref[...] = _layer_norm(alpha * x_ref[...] + f_ref[...], lw_ref[...], lb_ref[...])


def _res_ln(x, f, lw, lb, alpha):
    m, d = x.shape
    bm = 512
    row = pl.BlockSpec((bm, d), lambda i: (i, 0))
    full = pl.BlockSpec((1, d), lambda i: (0, 0))
    return pl.pallas_call(
        functools.partial(_res_ln_kernel, alpha=alpha),
        grid=(m // bm,),
        in_specs=[row, row, full, full],
        out_specs=row,
        out_shape=jax.ShapeDtypeStruct((m, d), F32),
        compiler_params=_params(("parallel",)),
        name="res_ln",
    )(x, f, lw.reshape(1, d), lb.reshape(1, d))


def _layer(x, w_in, sb_norm_w, ret_norm_w, w_out, ln1_w, ln1_b, w_pq, sub_keys, expert_u, expert_v,
           ln2_w, ln2_b, alpha):
    b, s, d = x.shape
    n_tok = b * s
    sb_heads = sb_norm_w.shape[0] // HEAD_DIM
    ret_heads = ret_norm_w.shape[0] // HEAD_DIM
    xt = x.reshape(n_tok, d)
    proj = _matmul(xt.astype(BF16), w_in.astype(BF16), 1024, 1024, BF16).reshape(b, s, -1)
    sb = _sb_attention(proj, sb_norm_w, sb_heads, 0)
    ret = _retention(proj, ret_norm_w, ret_heads, 3 * sb_heads)
    x1, x1b = _outproj_ln(sb.reshape(n_tok, -1), ret.reshape(n_tok, -1), w_out.astype(BF16), xt,
                          ln1_w, ln1_b, alpha)
    qp = _matmul(x1b, w_pq.astype(BF16), 1024, 1024, BF16)
    cnt, e1, rank2, e2 = _router(qp, sub_keys.astype(BF16), 512)
    ffn = _peer_experts(x1b, expert_u.astype(BF16), expert_v.astype(BF16), cnt, e1, rank2, e2, 512, 512)
    return _res_ln(x1, ffn, ln2_w, ln2_b, alpha).reshape(b, s, d)


def kernel(x, w_in, sb_norm_w, ret_norm_w, w_out, ln1_w, ln1_b, w_pq, sub_keys, expert_u, expert_v, ln2_w, ln2_b):
    depth = w_in.shape[0]
    alpha = (2.0 * depth) ** 0.25
    for layer in range(depth):
        x = _layer(x, w_in[layer], sb_norm_w[layer], ret_norm_w[layer], w_out[layer], ln1_w[layer],
                   ln1_b[layer], w_pq[layer], sub_keys[layer], expert_u[layer], expert_v[layer],
                   ln2_w[layer], ln2_b[layer], alpha)
    return x
```

```python
import functools
import math

import jax
import jax.numpy as jnp
from jax import lax
from jax.experimental import pallas as pl
from jax.experimental.pallas import tpu as pltpu

F32 = jnp.float32
BF16 = jnp.bfloat16

LANES = 128
F32_SUBLANES = 8
HEAD_DIM = 128
CHUNK = 64
ROPE_THETA = 10000.0
PEER_HEADS = 8
PEER_KEYS = 128
PEER_TOPK = 16
LN_EPS = 1e-5
NORM_EPS = 1e-6
SB_Q_SCALE = HEAD_DIM ** -0.5 * math.log2(math.e)

V7X_VMEM_BYTES = 64 * 1024 * 1024
V7X_MXU_WIDTH = 256
VMEM_LIMIT = 52 * 1024 * 1024

NT_DIMS = (((1,), (1,)), ((), ()))
TN_DIMS = (((0,), (0,)), ((), ()))


def _params(semantics):
    return pltpu.CompilerParams(dimension_semantics=semantics, vmem_limit_bytes=VMEM_LIMIT)


def _matmul_kernel(x_ref, w_ref, o_ref):
    o_ref[...] = jnp.dot(x_ref[...], w_ref[...], preferred_element_type=F32).astype(o_ref.dtype)


def _matmul(x, w, bm, bn, out_dtype):
    m, k = x.shape
    n = w.shape[1]
    return pl.pallas_call(
        _matmul_kernel,
        grid=(n // bn, m // bm),
        in_specs=[pl.BlockSpec((bm, k), lambda j, i: (i, 0)),
                  pl.BlockSpec((k, bn), lambda j, i: (0, j))],
        out_specs=pl.BlockSpec((bm, bn), lambda j, i: (i, j)),
        out_shape=jax.ShapeDtypeStruct((m, n), out_dtype),
        compiler_params=_params(("parallel", "parallel")),
        name="dense_proj",
    )(x, w)


def _sb_kernel(q_ref, k_ref, v_ref, tri_ref, w_ref, o_ref, acc_ref, run_ref, *, t, n_group):
    qi = pl.program_id(2)
    tri = tri_ref[...]
    below_diag = (lax.broadcasted_iota(jnp.int32, (t, t), 1) < lax.broadcasted_iota(jnp.int32, (t, t), 0))
    lanes = [slice(g * HEAD_DIM, (g + 1) * HEAD_DIM) for g in range(n_group)]

    def block(j, on_diagonal):
        start = pl.multiple_of(j * t, t)

        def scores(g):
            return lax.dot_general(q_ref[0, :, lanes[g]], k_ref[0, pl.ds(start, t), lanes[g]], NT_DIMS,
                                   preferred_element_type=F32)

        def suffix_sums(z):
            softplus = jnp.maximum(z, 0.0) + jnp.log2(1.0 + jnp.exp2(-jnp.abs(z)))
            log_beta = z - softplus
            if on_diagonal:
                softplus = jnp.where(below_diag, softplus, 0.0)
            hi = softplus.astype(BF16)
            lo = (softplus - hi.astype(F32)).astype(BF16)
            suffix = jnp.dot(jnp.concatenate([hi, lo], axis=1), tri, preferred_element_type=F32)
            return log_beta, suffix, jnp.sum(softplus, axis=1, keepdims=True)

        def weighted_values(g, log_beta, suffix, total):
            vs = v_ref[0, pl.ds(start, t), lanes[g]]
            if on_diagonal:
                a = jnp.where(below_diag, jnp.exp2(log_beta - suffix), 0.0)
                acc_ref[:, lanes[g]] = jnp.dot(a.astype(BF16), vs, preferred_element_type=F32)
                run_ref[g] = total
            else:
                run = run_ref[g]
                a = jnp.exp2(log_beta - suffix - run)
                acc_ref[:, lanes[g]] += jnp.dot(a.astype(BF16), vs, preferred_element_type=F32)
                run_ref[g] = run + total

        z, mid = {}, {}
        for step in range(n_group + 2):
            if step < n_group:
                z[step] = scores(step)
            if 0 <= step - 1 < n_group:
                mid[step - 1] = suffix_sums(z.pop(step - 1))
            if 0 <= step - 2 < n_group:
                weighted_values(step - 2, *mid.pop(step - 2))

    block(qi, True)

    def off_diagonal(jj, _):
        block(qi - jj, False)
        return 0

    lax.fori_loop(1, qi + 1, off_diagonal, 0)
    for g in range(n_group):
        acc = acc_ref[:, lanes[g]]
        ms = jnp.mean(acc * acc, axis=-1, keepdims=True)
        o_ref[0, :, lanes[g]] = (acc * lax.rsqrt(ms + NORM_EPS) * w_ref[:, lanes[g]]).astype(o_ref.dtype)


def _sb_attention(proj, norm_w, n_heads, col0, n_group=8, t=256):
    b, s, _ = proj.shape
    width = n_group * HEAD_DIM
    strict_lower = (jnp.arange(t)[:, None] > jnp.arange(t)[None, :]).astype(BF16)
    tri = jnp.concatenate([strict_lower, strict_lower], axis=0)
    kern = functools.partial(_sb_kernel, t=t, n_group=n_group)
    group_col = lambda part: (col0 + part * n_heads) // n_group
    return pl.pallas_call(
        kern,
        grid=(b, n_heads // n_group, s // t),
        in_specs=[pl.BlockSpec((1, t, width), lambda bi, hg, i: (bi, i, group_col(0) + hg)),
                  pl.BlockSpec((1, s, width), lambda bi, hg, i: (bi, 0, group_col(1) + hg)),
                  pl.BlockSpec((1, s, width), lambda bi, hg, i: (bi, 0, group_col(2) + hg)),
                  pl.BlockSpec((2 * t, t), lambda bi, hg, i: (0, 0)),
                  pl.BlockSpec((1, width), lambda bi, hg, i: (0, hg))],
        out_specs=pl.BlockSpec((1, t, width), lambda bi, hg, i: (bi, i, hg)),
        out_shape=jax.ShapeDtypeStruct((b, s, n_heads * HEAD_DIM), BF16),
        scratch_shapes=[pltpu.VMEM((t, width), F32), pltpu.VMEM((n_group, t, 1), F32)],
        compiler_params=_params(("parallel", "parallel", "arbitrary")),
        name="sb_attention",
    )(proj, proj, proj, tri, norm_w.reshape(1, -1).astype(F32))


def _ret_kernel(q_ref, k_ref, v_ref, g_ref, cos_ref, sin_ref, dec_ref, xi_ref, zeta_ref, gt_ref, w_ref,
                o_ref, state_ref):
    @pl.when(pl.program_id(2) == 0)
    def _():
        state_ref[...] = jnp.zeros_like(state_ref)

    cos = cos_ref[...]
    sin = sin_ref[...]
    q = q_ref[0].astype(F32)
    k = k_ref[0].astype(F32)
    half = HEAD_DIM // 2
    qr = q * cos + pltpu.roll(q, half, 1) * sin
    kr = k * cos + pltpu.roll(k, half, 1) * sin
    qb = qr.astype(BF16)
    v = v_ref[0]
    state = state_ref[...]
    scores = lax.dot_general(qb, kr.astype(BF16), NT_DIMS, preferred_element_type=F32) * dec_ref[0]
    ret = jnp.dot(scores.astype(BF16), v, preferred_element_type=F32)
    ret = ret + jnp.dot(qb, state.astype(BF16), preferred_element_type=F32) * xi_ref[0]
    kz = (kr * zeta_ref[0]).astype(BF16)
    state_ref[...] = state * gt_ref[0] + lax.dot_general(kz, v, TN_DIMS, preferred_element_type=F32)
    mu = jnp.mean(ret, axis=-1, keepdims=True)
    cen = ret - mu
    var = jnp.mean(cen * cen, axis=-1, keepdims=True)
    g = g_ref[0].astype(F32)
    silu = g / (1.0 + jnp.exp(-g))
    o_ref[0] = (cen * lax.rsqrt(var + NORM_EPS) * w_ref[...] * silu).astype(o_ref.dtype)


def _retention(proj, norm_w, n_heads, col0):
    b, s, _ = proj.shape
    t = 256
    d = HEAD_DIM
    inv_freq = ROPE_THETA ** (-jnp.arange(0, d, 2, dtype=F32) / d)
    ang = jnp.arange(s, dtype=F32)[:, None] * inv_freq[None, :]
    ang = jnp.concatenate([ang, ang], -1)
    sign = jnp.concatenate([-jnp.ones((d // 2,), F32), jnp.ones((d // 2,), F32)])
    cos_t = jnp.cos(ang)
    sin_t = jnp.sin(ang) * sign[None, :]
    log_g = jnp.log1p(-(2.0 ** (-5.0 - jnp.arange(n_heads, dtype=F32))))
    idx = jnp.arange(t, dtype=F32)
    same_or_earlier_chunk = (jnp.floor(idx[None, :] / CHUNK) <= jnp.floor(idx[:, None] / CHUNK))
    dec = jnp.exp(log_g[:, None, None] * jnp.abs(idx[:, None] - idx[None, :])) * (d ** -0.5)
    dec = jnp.where(same_or_earlier_chunk[None], dec, 0.0)
    xi = jnp.broadcast_to(jnp.exp(log_g[:, None] * (idx + 1.0))[:, :, None], (n_heads, t, d))
    zeta = jnp.broadcast_to((jnp.exp(log_g[:, None] * (t - 1.0 - idx)) * (d ** -0.5))[:, :, None], (n_heads, t, d))
    gt = jnp.broadcast_to(jnp.exp(log_g * t)[:, None, None], (n_heads, 1, d))
    head_spec = lambda off: pl.BlockSpec((1, t, d), lambda bi, h, i: (bi, i, col0 + off * n_heads + h))
    return pl.pallas_call(
        _ret_kernel,
        grid=(b, n_heads, s // t),
        in_specs=[head_spec(0), head_spec(1), head_spec(2), head_spec(3),
                  pl.BlockSpec((t, d), lambda bi, h, i: (i, 0)),
                  pl.BlockSpec((t, d), lambda bi, h, i: (i, 0)),
                  pl.BlockSpec((1, t, t), lambda bi, h, i: (h, 0, 0)),
                  pl.BlockSpec((1, t, d), lambda bi, h, i: (h, 0, 0)),
                  pl.BlockSpec((1, t, d), lambda bi, h, i: (h, 0, 0)),
                  pl.BlockSpec((1, 1, d), lambda bi, h, i: (h, 0, 0)),
                  pl.BlockSpec((1, d), lambda bi, h, i: (0, h))],
        out_specs=pl.BlockSpec((1, t, d), lambda bi, h, i: (bi, i, h)),
        out_shape=jax.ShapeDtypeStruct((b, s, n_heads * d), BF16),
        scratch_shapes=[pltpu.VMEM((d, d), F32)],
        compiler_params=_params(("parallel", "parallel", "arbitrary")),
        name="retention",
    )(proj, proj, proj, proj, cos_t, sin_t, dec, xi, zeta, gt, norm_w.reshape(1, -1).astype(F32))


def _layer_norm(y, w, b):
    mu = jnp.mean(y, axis=-1, keepdims=True)
    cen = y - mu
    var = jnp.mean(cen * cen, axis=-1, keepdims=True)
    return cen * lax.rsqrt(var + LN_EPS) * w + b


def _outproj_ln_kernel(sb_ref, ret_ref, wo_ref, x_ref, lw_ref, lb_ref, o_ref, ob_ref, *, alpha, split):
    mix = jnp.dot(sb_ref[...], wo_ref[:split, :], preferred_element_type=F32)
    mix = mix + jnp.dot(ret_ref[...], wo_ref[split:, :], preferred_element_type=F32)
    y = _layer_norm(alpha * x_ref[...] + mix, lw_ref[...], lb_ref[...])
    o_ref[...] = y
    ob_ref[...] = y.astype(BF16)


def _outproj_ln(sb, ret, wo, x, lw, lb, alpha):
    m, d = x.shape
    split = sb.shape[1]
    bm = 256
    kern = functools.partial(_outproj_ln_kernel, alpha=alpha, split=split)
    row = lambda width: pl.BlockSpec((bm, width), lambda i: (i, 0))
    full = lambda shape: pl.BlockSpec(shape, lambda i: (0, 0))
    return pl.pallas_call(
        kern,
        grid=(m // bm,),
        in_specs=[row(split), row(ret.shape[1]), full(wo.shape), row(d), full((1, d)), full((1, d))],
        out_specs=[row(d), row(d)],
        out_shape=[jax.ShapeDtypeStruct((m, d), F32), jax.ShapeDtypeStruct((m, d), BF16)],
        compiler_params=_params(("parallel",)),
        name="outproj_ln",
    )(sb, ret, wo, x, lw.reshape(1, d), lb.reshape(1, d))


def _top_ranks(s, iota, k):
    n = s.shape[0]
    work = s
    rank = jnp.full(s.shape, float(k), F32)
    tops = []
    for r in range(k):
        m = jnp.max(work, axis=0, keepdims=True)
        first = jnp.min(jnp.where(work == m, iota, float(n)), axis=0, keepdims=True)
        sel = iota == first
        rank = jnp.where(sel, float(r), rank)
        work = jnp.where(sel, -jnp.inf, work)
        tops.append(m)
    return rank, tops


def _router_kernel(q_ref, keys_ref, cnt_ref, e1_ref, rank2_ref, e2_ref, *, lanes):
    nk = PEER_KEYS
    k = PEER_TOPK
    iota = lax.broadcasted_iota(jnp.int32, (nk, lanes), 0).astype(F32)
    widths = [k // (a + 1) for a in range(k)]
    n_cand = sum(widths)
    n_pad = -(-n_cand // 8) * 8
    ciota = lax.broadcasted_iota(jnp.int32, (n_pad, lanes), 0).astype(F32)

    def column(c, _):
        sl = pl.ds(pl.multiple_of(c * lanes, lanes), lanes)
        q = q_ref[sl, :]
        half = q.shape[1] // 2
        s1 = lax.dot_general(keys_ref[0, 0], q[:, :half], NT_DIMS, preferred_element_type=F32)
        s2 = lax.dot_general(keys_ref[0, 1], q[:, half:], NT_DIMS, preferred_element_type=F32)
        rank1, top1 = _top_ranks(s1, iota, k)
        rank2, top2 = _top_ranks(s2, iota, k)
        top2_all = jnp.concatenate(top2, axis=0)
        cand = [top1[a] + top2_all[:widths[a]] for a in range(k)]
        if n_pad > n_cand:
            cand.append(jnp.full((n_pad - n_cand, lanes), -jnp.inf, F32))
        cand = jnp.concatenate(cand, axis=0)
        work = cand
        chosen = jnp.zeros(cand.shape, F32)
        for _ in range(k):
            m = jnp.max(work, axis=0, keepdims=True)
            first = jnp.min(jnp.where(work == m, ciota, float(n_pad)), axis=0, keepdims=True)
            sel = ciota == first
            chosen = jnp.where(sel, 1.0, chosen)
            work = jnp.where(sel, -jnp.inf, work)
        best = top1[0] + top2[0]
        z = jnp.sum(jnp.where(chosen > 0.0, jnp.exp(cand - best), 0.0), axis=0, keepdims=True)
        cnt = jnp.zeros((nk, lanes), F32)
        off = 0
        for a in range(k):
            cnt_a = jnp.sum(chosen[off:off + widths[a]], axis=0, keepdims=True)
            cnt = jnp.where(rank1 == float(a), cnt_a, cnt)
            off += widths[a]
        cnt_ref[0, c] = cnt
        e1_ref[0, c] = jnp.exp(s1 - top1[0])
        rank2_ref[0, c] = rank2
        e2_ref[0, c] = jnp.exp(s2 - top2[0]) / z
        return 0

    lax.fori_loop(0, q_ref.shape[0] // lanes, column, 0)


def _router(qp, sub_keys, tb):
    t = qp.shape[0]
    h, _, nk, half = sub_keys.shape
    out_f32 = jax.ShapeDtypeStruct((h, t // LANES, nk, LANES), F32)
    out_spec = pl.BlockSpec((1, tb // LANES, nk, LANES), lambda j, hh: (hh, j, 0, 0))
    return pl.pallas_call(
        functools.partial(_router_kernel, lanes=LANES),
        grid=(t // tb, h),
        in_specs=[pl.BlockSpec((tb, 2 * half), lambda j, hh: (j, hh)),
                  pl.BlockSpec((1, 2, nk, half), lambda j, hh: (hh, 0, 0, 0))],
        out_specs=[out_spec, out_spec, out_spec, out_spec],
        out_shape=[out_f32, out_f32, out_f32, out_f32],
        compiler_params=_params(("parallel", "parallel")),
        name="peer_router",
    )(qp, sub_keys)


def _row_to_packed_tile(ref, h, c, row, n_rows):
    rep = jnp.broadcast_to(ref[h, c, pl.ds(row, 1), :], (2 * F32_SUBLANES, LANES))
    packed = rep.astype(BF16)
    return pltpu.repeat(packed, n_rows // (2 * F32_SUBLANES), axis=0)


def _peer_kernel(x_ref, u_ref, v_ref, cnt_ref, e1_ref, rank2_in, e2_in, o_ref,
                 acc_ref, act_a, act_b, hid_a, hid_b, rank2_ref, e2_ref, *, n_sub, n_e, n_items):
    s = pl.program_id(0)
    vec_e = jnp.clip(s - 1, 0, n_items - 1) % n_e
    out_e = jnp.clip(s - 2, 0, n_items - 1) % n_e
    tb = x_ref.shape[0]
    nk = PEER_KEYS

    @pl.when(s == 0)
    def _():
        for ref in (act_a, act_b, hid_a, hid_b):
            ref[...] = jnp.zeros_like(ref)

    @pl.when(vec_e == 0)
    def _():
        rank2_ref[...] = rank2_in[...].astype(BF16)
        e2_ref[...] = e2_in[...].astype(BF16)

    @pl.when(out_e == 0)
    def _():
        acc_ref[...] = jnp.zeros_like(acc_ref)

    def gated_tile(act_r, hid_w, ii, c):
        i1 = vec_e * n_sub + ii
        gate = jnp.zeros((nk, LANES), BF16)
        for h in range(PEER_HEADS):
            cnt = _row_to_packed_tile(cnt_ref, h, c, i1, nk)
            e1 = _row_to_packed_tile(e1_ref, h, c, i1, nk)
            gate = gate + jnp.where(rank2_ref[h, c] < cnt, e2_ref[h, c] * e1, jnp.zeros((), BF16))
        a = act_r[ii * nk:(ii + 1) * nk, c * LANES:(c + 1) * LANES]
        gelu = 0.5 * a * (1.0 + lax.erf(a * (2.0 ** -0.5)))
        hid_w[c * LANES:(c + 1) * LANES, ii * nk:(ii + 1) * nk] = (gelu.astype(BF16) * gate).T

    def stages(act_w, act_r, hid_w, hid_r):
        half = tb // 2
        for p in range(2):
            act_w[:, p * half:(p + 1) * half] = lax.dot_general(
                u_ref[...], x_ref[p * half:(p + 1) * half, :], NT_DIMS, preferred_element_type=F32)
        tiles = [(ii, c) for ii in range(n_sub) for c in range(tb // LANES)]
        n_chunks = v_ref.shape[1] // V7X_MXU_WIDTH
        per_chunk = -(-len(tiles) // n_chunks)
        for k in range(n_chunks):
            cols = slice(k * V7X_MXU_WIDTH, (k + 1) * V7X_MXU_WIDTH)
            acc_ref[:, cols] += jnp.dot(hid_r[...], v_ref[:, cols], preferred_element_type=F32)
            for ii, c in tiles[k * per_chunk:(k + 1) * per_chunk]:
                gated_tile(act_r, hid_w, ii, c)

    @pl.when(s % 2 == 0)
    def _():
        stages(act_a, act_b, hid_b, hid_a)

    @pl.when(s % 2 == 1)
    def _():
        stages(act_b, act_a, hid_a, hid_b)

    @pl.when((s >= 2) & (out_e == n_e - 1))
    def _():
        o_ref[...] = acc_ref[...]


def _peer_experts(xb, u, v, cnt, e1, rank2, e2, tb, eb):
    t, d = xb.shape
    n_exp = u.shape[0]
    h, _, nk, _ = cnt.shape
    n_e = n_exp // eb
    n_items = (t // tb) * n_e
    assert n_items % 2 == 0
    mm_item = lambda s: jnp.minimum(s, n_items - 1)
    vec_item = lambda s: jnp.clip(s - 1, 0, n_items - 1)
    out_item = lambda s: jnp.clip(s - 2, 0, n_items - 1)
    route = pl.BlockSpec((h, tb // LANES, nk, LANES), lambda s: (0, vec_item(s) // n_e, 0, 0))
    packed = pltpu.VMEM((h, tb // LANES, nk, LANES), BF16)
    return pl.pallas_call(
        functools.partial(_peer_kernel, n_sub=eb // nk, n_e=n_e, n_items=n_items),
        grid=(n_items + 2,),
        in_specs=[pl.BlockSpec((tb, d), lambda s: (mm_item(s) // n_e, 0)),
                  pl.BlockSpec((eb, d), lambda s: (mm_item(s) % n_e, 0)),
                  pl.BlockSpec((eb, d), lambda s: (out_item(s) % n_e, 0)),
                  route, route, route, route],
        out_specs=pl.BlockSpec((tb, d), lambda s: (out_item(s) // n_e, 0)),
        out_shape=jax.ShapeDtypeStruct((t, d), F32),
        scratch_shapes=[pltpu.VMEM((tb, d), F32),
                        pltpu.VMEM((eb, tb), F32), pltpu.VMEM((eb, tb), F32),
                        pltpu.VMEM((tb, eb), BF16), pltpu.VMEM((tb, eb), BF16),
                        packed, packed],
        compiler_params=_params(("arbitrary",)),
        name="peer_experts",
    )(xb, u, v, cnt, e1, rank2, e2)


def _res_ln_kernel(x_ref, f_ref, lw_ref, lb_ref, o_ref, *, alpha):
    o_ref[...] = _layer_norm(alpha * x_ref[...] + f_ref[...], lw_ref[...], lb_ref[...])


def _res_ln(x, f, lw, lb, alpha):
    m, d = x.shape
    bm = 512
    row = pl.BlockSpec((bm, d), lambda i: (i, 0))
    full = pl.BlockSpec((1, d), lambda i: (0, 0))
    return pl.pallas_call(
        functools.partial(_res_ln_kernel, alpha=alpha),
        grid=(m // bm,),
        in_specs=[row, row, full, full],
        out_specs=row,
        out_shape=jax.ShapeDtypeStruct((m, d), F32),
        compiler_params=_params(("parallel",)),
        name="res_ln",
    )(x, f, lw.reshape(1, d), lb.reshape(1, d))


def _layer(x, w_in, sb_norm_w, ret_norm_w, w_out, ln1_w, ln1_b, w_pq, sub_keys, expert_u, expert_v,
           ln2_w, ln2_b, alpha):
    b, s, d = x.shape
    n_tok = b * s
    sb_heads = sb_norm_w.shape[0] // HEAD_DIM
    ret_heads = ret_norm_w.shape[0] // HEAD_DIM
    xt = x.reshape(n_tok, d)
    col_scale = jnp.where(jnp.arange(w_in.shape[1]) < sb_heads * HEAD_DIM, SB_Q_SCALE, 1.0).astype(F32)
    proj = _matmul(xt.astype(BF16), (w_in * col_scale).astype(BF16), 1024, 1024, BF16).reshape(b, s, -1)
    sb = _sb_attention(proj, sb_norm_w, sb_heads, 0)
    ret = _retention(proj, ret_norm_w, ret_heads, 3 * sb_heads)
    x1, x1b = _outproj_ln(sb.reshape(n_tok, -1), ret.reshape(n_tok, -1), w_out.astype(BF16), xt,
                          ln1_w, ln1_b, alpha)
    qp = _matmul(x1b, w_pq.astype(BF16), 1024, 1024, BF16)
    cnt, e1, rank2, e2 = _router(qp, sub_keys.astype(BF16), 512)
    ffn = _peer_experts(x1b, expert_u.astype(BF16), expert_v.astype(BF16), cnt, e1, rank2, e2, 512, 512)
    return _res_ln(x1, ffn, ln2_w, ln2_b, alpha).reshape(b, s, d)


def kernel(x, w_in, sb_norm_w, ret_norm_w, w_out, ln1_w, ln1_b, w_pq, sub_keys, expert_u, expert_v, ln2_w, ln2_b):
    depth = w_in.shape[0]
    alpha = (2.0 * depth) ** 0.25
    for layer in range(depth):
        x = _layer(x, w_in[layer], sb_norm_w[layer], ret_norm_w[layer], w_out[layer], ln1_w[layer],
                   ln1_b[layer], w_pq[layer], sub_keys[layer], expert_u[layer], expert_v[layer],
                   ln2_w[layer], ln2_b[layer], alpha)
    return x
```

```python
import functools
import math

import jax
import jax.numpy as jnp
from jax import lax
from jax.experimental import pallas as pl
from jax.experimental.pallas import tpu as pltpu

F32 = jnp.float32
BF16 = jnp.bfloat16

LANES = 128
F32_SUBLANES = 8
BF16_SUBLANES = 16
HEAD_DIM = 128
CHUNK = 64
ROPE_THETA = 10000.0
PEER_HEADS = 8
PEER_KEYS = 128
PEER_TOPK = 16
LN_EPS = 1e-5
NORM_EPS = 1e-6
SB_Q_SCALE = HEAD_DIM ** -0.5 * math.log2(math.e)

V7X_VMEM_BYTES = 64 * 1024 * 1024
V7X_MXU_WIDTH = 256
VMEM_LIMIT = 52 * 1024 * 1024

NT_DIMS = (((1,), (1,)), ((), ()))
TN_DIMS = (((0,), (0,)), ((), ()))


def _params(semantics):
    return pltpu.CompilerParams(dimension_semantics=semantics, vmem_limit_bytes=VMEM_LIMIT)


def _matmul_kernel(x_ref, w_ref, o_ref):
    o_ref[...] = jnp.dot(x_ref[...], w_ref[...], preferred_element_type=F32).astype(o_ref.dtype)


def _matmul(x, w, bm, bn, out_dtype):
    m, k = x.shape
    n = w.shape[1]
    return pl.pallas_call(
        _matmul_kernel,
        grid=(n // bn, m // bm),
        in_specs=[pl.BlockSpec((bm, k), lambda j, i: (i, 0)),
                  pl.BlockSpec((k, bn), lambda j, i: (0, j))],
        out_specs=pl.BlockSpec((bm, bn), lambda j, i: (i, j)),
        out_shape=jax.ShapeDtypeStruct((m, n), out_dtype),
        compiler_params=_params(("parallel", "parallel")),
        name="dense_proj",
    )(x, w)


def _sb_kernel(q_ref, k_ref, v_ref, tri_ref, w_ref, o_ref, acc_ref, run_ref, *, t, n_group):
    qi = pl.program_id(2)
    tri = tri_ref[...]
    below_diag = (lax.broadcasted_iota(jnp.int32, (t, t), 1) < lax.broadcasted_iota(jnp.int32, (t, t), 0))
    lanes = [slice(g * HEAD_DIM, (g + 1) * HEAD_DIM) for g in range(n_group)]

    def block(j, on_diagonal):
        start = pl.multiple_of(j * t, t)

        def scores(g):
            return lax.dot_general(q_ref[0, :, lanes[g]], k_ref[0, pl.ds(start, t), lanes[g]], NT_DIMS,
                                   preferred_element_type=F32)

        def suffix_sums(z):
            softplus = jnp.maximum(z, 0.0) + jnp.log2(1.0 + jnp.exp2(-jnp.abs(z)))
            log_beta = z - softplus
            if on_diagonal:
                softplus = jnp.where(below_diag, softplus, 0.0)
            hi = softplus.astype(BF16)
            lo = (softplus - hi.astype(F32)).astype(BF16)
            suffix = jnp.dot(jnp.concatenate([hi, lo], axis=1), tri, preferred_element_type=F32)
            return log_beta, suffix, jnp.sum(softplus, axis=1, keepdims=True)

        def weighted_values(g, log_beta, suffix, total):
            vs = v_ref[0, pl.ds(start, t), lanes[g]]
            if on_diagonal:
                a = jnp.where(below_diag, jnp.exp2(log_beta - suffix), 0.0)
                acc_ref[:, lanes[g]] = jnp.dot(a.astype(BF16), vs, preferred_element_type=F32)
                run_ref[g] = total
            else:
                run = run_ref[g]
                a = jnp.exp2(log_beta - suffix - run)
                acc_ref[:, lanes[g]] += jnp.dot(a.astype(BF16), vs, preferred_element_type=F32)
                run_ref[g] = run + total

        z, mid = {}, {}
        for step in range(n_group + 2):
            if step < n_group:
                z[step] = scores(step)
            if 0 <= step - 1 < n_group:
                mid[step - 1] = suffix_sums(z.pop(step - 1))
            if 0 <= step - 2 < n_group:
                weighted_values(step - 2, *mid.pop(step - 2))

    block(qi, True)

    def off_diagonal(jj, _):
        block(qi - jj, False)
        return 0

    lax.fori_loop(1, qi + 1, off_diagonal, 0)
    for g in range(n_group):
        acc = acc_ref[:, lanes[g]]
        ms = jnp.mean(acc * acc, axis=-1, keepdims=True)
        o_ref[0, :, lanes[g]] = (acc * lax.rsqrt(ms + NORM_EPS) * w_ref[:, lanes[g]]).astype(o_ref.dtype)


def _sb_attention(proj, norm_w, n_heads, col0, n_group=8, t=256):
    b, s, _ = proj.shape
    width = n_group * HEAD_DIM
    strict_lower = (jnp.arange(t)[:, None] > jnp.arange(t)[None, :]).astype(BF16)
    tri = jnp.concatenate([strict_lower, strict_lower], axis=0)
    kern = functools.partial(_sb_kernel, t=t, n_group=n_group)
    group_col = lambda part: (col0 + part * n_heads) // n_group
    return pl.pallas_call(
        kern,
        grid=(b, n_heads // n_group, s // t),
        in_specs=[pl.BlockSpec((1, t, width), lambda bi, hg, i: (bi, i, group_col(0) + hg)),
                  pl.BlockSpec((1, s, width), lambda bi, hg, i: (bi, 0, group_col(1) + hg)),
                  pl.BlockSpec((1, s, width), lambda bi, hg, i: (bi, 0, group_col(2) + hg)),
                  pl.BlockSpec((2 * t, t), lambda bi, hg, i: (0, 0)),
                  pl.BlockSpec((1, width), lambda bi, hg, i: (0, hg))],
        out_specs=pl.BlockSpec((1, t, width), lambda bi, hg, i: (bi, i, hg)),
        out_shape=jax.ShapeDtypeStruct((b, s, n_heads * HEAD_DIM), BF16),
        scratch_shapes=[pltpu.VMEM((t, width), F32), pltpu.VMEM((n_group, t, 1), F32)],
        compiler_params=_params(("parallel", "parallel", "arbitrary")),
        name="sb_attention",
    )(proj, proj, proj, tri, norm_w.reshape(1, -1).astype(F32))


def _ret_kernel(q_ref, k_ref, v_ref, g_ref, cos_ref, sin_ref, dec_ref, xi_ref, zeta_ref, gt_ref, w_ref,
                o_ref, state_ref):
    @pl.when(pl.program_id(2) == 0)
    def _():
        state_ref[...] = jnp.zeros_like(state_ref)

    cos = cos_ref[...]
    sin = sin_ref[...]
    q = q_ref[0].astype(F32)
    k = k_ref[0].astype(F32)
    half = HEAD_DIM // 2
    qr = q * cos + pltpu.roll(q, half, 1) * sin
    kr = k * cos + pltpu.roll(k, half, 1) * sin
    qb = qr.astype(BF16)
    v = v_ref[0]
    state = state_ref[...]
    scores = lax.dot_general(qb, kr.astype(BF16), NT_DIMS, preferred_element_type=F32) * dec_ref[0]
    ret = jnp.dot(scores.astype(BF16), v, preferred_element_type=F32)
    ret = ret + jnp.dot(qb, state.astype(BF16), preferred_element_type=F32) * xi_ref[0]
    kz = (kr * zeta_ref[0]).astype(BF16)
    state_ref[...] = state * gt_ref[0] + lax.dot_general(kz, v, TN_DIMS, preferred_element_type=F32)
    mu = jnp.mean(ret, axis=-1, keepdims=True)
    cen = ret - mu
    var = jnp.mean(cen * cen, axis=-1, keepdims=True)
    g = g_ref[0].astype(F32)
    silu = g / (1.0 + jnp.exp(-g))
    o_ref[0] = (cen * lax.rsqrt(var + NORM_EPS) * w_ref[...] * silu).astype(o_ref.dtype)


def _retention(proj, norm_w, n_heads, col0):
    b, s, _ = proj.shape
    t = 256
    d = HEAD_DIM
    inv_freq = ROPE_THETA ** (-jnp.arange(0, d, 2, dtype=F32) / d)
    ang = jnp.arange(s, dtype=F32)[:, None] * inv_freq[None, :]
    ang = jnp.concatenate([ang, ang], -1)
    sign = jnp.concatenate([-jnp.ones((d // 2,), F32), jnp.ones((d // 2,), F32)])
    cos_t = jnp.cos(ang)
    sin_t = jnp.sin(ang) * sign[None, :]
    log_g = jnp.log1p(-(2.0 ** (-5.0 - jnp.arange(n_heads, dtype=F32))))
    idx = jnp.arange(t, dtype=F32)
    same_or_earlier_chunk = (jnp.floor(idx[None, :] / CHUNK) <= jnp.floor(idx[:, None] / CHUNK))
    dec = jnp.exp(log_g[:, None, None] * jnp.abs(idx[:, None] - idx[None, :])) * (d ** -0.5)
    dec = jnp.where(same_or_earlier_chunk[None], dec, 0.0)
    xi = jnp.broadcast_to(jnp.exp(log_g[:, None] * (idx + 1.0))[:, :, None], (n_heads, t, d))
    zeta = jnp.broadcast_to((jnp.exp(log_g[:, None] * (t - 1.0 - idx)) * (d ** -0.5))[:, :, None], (n_heads, t, d))
    gt = jnp.broadcast_to(jnp.exp(log_g * t)[:, None, None], (n_heads, 1, d))
    head_spec = lambda off: pl.BlockSpec((1, t, d), lambda bi, h, i: (bi, i, col0 + off * n_heads + h))
    return pl.pallas_call(
        _ret_kernel,
        grid=(b, n_heads, s // t),
        in_specs=[head_spec(0), head_spec(1), head_spec(2), head_spec(3),
                  pl.BlockSpec((t, d), lambda bi, h, i: (i, 0)),
                  pl.BlockSpec((t, d), lambda bi, h, i: (i, 0)),
                  pl.BlockSpec((1, t, t), lambda bi, h, i: (h, 0, 0)),
                  pl.BlockSpec((1, t, d), lambda bi, h, i: (h, 0, 0)),
                  pl.BlockSpec((1, t, d), lambda bi, h, i: (h, 0, 0)),
                  pl.BlockSpec((1, 1, d), lambda bi, h, i: (h, 0, 0)),
                  pl.BlockSpec((1, d), lambda bi, h, i: (0, h))],
        out_specs=pl.BlockSpec((1, t, d), lambda bi, h, i: (bi, i, h)),
        out_shape=jax.ShapeDtypeStruct((b, s, n_heads * d), BF16),
        scratch_shapes=[pltpu.VMEM((d, d), F32)],
        compiler_params=_params(("parallel", "parallel", "arbitrary")),
        name="retention",
    )(proj, proj, proj, proj, cos_t, sin_t, dec, xi, zeta, gt, norm_w.reshape(1, -1).astype(F32))


def _layer_norm(y, w, b):
    mu = jnp.mean(y, axis=-1, keepdims=True)
    cen = y - mu
    var = jnp.mean(cen * cen, axis=-1, keepdims=True)
    return cen * lax.rsqrt(var + LN_EPS) * w + b


def _outproj_ln_kernel(sb_ref, ret_ref, wo_ref, x_ref, lw_ref, lb_ref, o_ref, ob_ref, *, alpha, split):
    mix = jnp.dot(sb_ref[...], wo_ref[:split, :], preferred_element_type=F32)
    mix = mix + jnp.dot(ret_ref[...], wo_ref[split:, :], preferred_element_type=F32)
    y = _layer_norm(alpha * x_ref[...] + mix, lw_ref[...], lb_ref[...])
    o_ref[...] = y
    ob_ref[...] = y.astype(BF16)


def _outproj_ln(sb, ret, wo, x, lw, lb, alpha):
    m, d = x.shape
    split = sb.shape[1]
    bm = 256
    kern = functools.partial(_outproj_ln_kernel, alpha=alpha, split=split)
    row = lambda width: pl.BlockSpec((bm, width), lambda i: (i, 0))
    full = lambda shape: pl.BlockSpec(shape, lambda i: (0, 0))
    return pl.pallas_call(
        kern,
        grid=(m // bm,),
        in_specs=[row(split), row(ret.shape[1]), full(wo.shape), row(d), full((1, d)), full((1, d))],
        out_specs=[row(d), row(d)],
        out_shape=[jax.ShapeDtypeStruct((m, d), F32), jax.ShapeDtypeStruct((m, d), BF16)],
        compiler_params=_params(("parallel",)),
        name="outproj_ln",
    )(sb, ret, wo, x, lw.reshape(1, d), lb.reshape(1, d))


def _top_ranks(s, iota, k, exact_ties):
    n = s.shape[0]
    work = s
    rank = jnp.full(s.shape, float(k), F32)
    tops = []
    for r in range(k):
        m = jnp.max(work, axis=0, keepdims=True)
        sel = work == m
        if exact_ties:
            first = jnp.min(jnp.where(sel, iota, float(n)), axis=0, keepdims=True)
            sel = iota == first
        rank = jnp.where(sel, float(r), rank)
        work = jnp.where(sel, -jnp.inf, work)
        tops.append(m)
    return rank, tops


def _route_column(s1, s2, iota, ciota, widths, exact_ties):
    k = PEER_TOPK
    nk, lanes = s1.shape
    n_cand = sum(widths)
    n_pad = ciota.shape[0]
    rank1, top1 = _top_ranks(s1, iota, k, exact_ties)
    rank2, top2 = _top_ranks(s2, iota, k, exact_ties)
    top2_all = jnp.concatenate(top2, axis=0)
    cand = [top1[a] + top2_all[:widths[a]] for a in range(k)]
    if n_pad > n_cand:
        cand.append(jnp.full((n_pad - n_cand, lanes), -jnp.inf, F32))
    cand = jnp.concatenate(cand, axis=0)
    crank, _ = _top_ranks(cand, ciota, k, exact_ties)
    chosen = crank < float(k)
    best = top1[0] + top2[0]
    z = jnp.sum(jnp.where(chosen, jnp.exp(cand - best), 0.0), axis=0, keepdims=True)
    ones = jnp.where(chosen, 1.0, 0.0)
    cnt = jnp.zeros((nk, lanes), F32)
    off = 0
    for a in range(k):
        cnt_a = jnp.sum(ones[off:off + widths[a]], axis=0, keepdims=True)
        cnt = jnp.where(rank1 == float(a), cnt_a, cnt)
        off += widths[a]
    picked = (jnp.sum(jnp.where(rank1 < float(k), 1.0, 0.0), axis=0, keepdims=True)
              + jnp.sum(jnp.where(rank2 < float(k), 1.0, 0.0), axis=0, keepdims=True)
              + jnp.sum(ones, axis=0, keepdims=True))
    unique = picked == float(3 * k)
    return (cnt, jnp.exp(s1 - top1[0]), rank2, jnp.exp(s2 - top2[0]) / z), unique


def _router_kernel(q_ref, keys_ref, cnt_ref, e1_ref, rank2_ref, e2_ref, *, lanes, cols_per_iter):
    nk = PEER_KEYS
    k = PEER_TOPK
    iota = lax.broadcasted_iota(jnp.int32, (nk, lanes), 0).astype(F32)
    widths = [k // (a + 1) for a in range(k)]
    n_pad = -(-sum(widths) // F32_SUBLANES) * F32_SUBLANES
    ciota = lax.broadcasted_iota(jnp.int32, (n_pad, lanes), 0).astype(F32)
    out_refs = (cnt_ref, e1_ref, rank2_ref, e2_ref)

    def columns(group, _):
        pending = []
        for j in range(cols_per_iter):
            c = group * cols_per_iter + j
            sl = pl.ds(pl.multiple_of(c * lanes, lanes), lanes)
            q = q_ref[sl, :]
            half = q.shape[1] // 2
            s1 = lax.dot_general(keys_ref[0, 0], q[:, :half], NT_DIMS, preferred_element_type=F32)
            s2 = lax.dot_general(keys_ref[0, 1], q[:, half:], NT_DIMS, preferred_element_type=F32)
            outs, unique = _route_column(s1, s2, iota, ciota, widths, exact_ties=False)
            for ref, val in zip(out_refs, outs):
                ref[0, c] = val
            pending.append((c, s1, s2, jnp.min(jnp.where(unique, 1.0, 0.0))))
        for c, s1, s2, all_unique in pending:
            @pl.when(all_unique < 0.5)
            def _():
                exact, _ = _route_column(s1, s2, iota, ciota, widths, exact_ties=True)
                for ref, val in zip(out_refs, exact):
                    ref[0, c] = val
        return 0

    lax.fori_loop(0, q_ref.shape[0] // (lanes * cols_per_iter), columns, 0)


def _router(qp, sub_keys, tb):
    t = qp.shape[0]
    h, _, nk, half = sub_keys.shape
    out_f32 = jax.ShapeDtypeStruct((h, t // LANES, nk, LANES), F32)
    out_spec = pl.BlockSpec((1, tb // LANES, nk, LANES), lambda j, hh: (hh, j, 0, 0))
    return pl.pallas_call(
        functools.partial(_router_kernel, lanes=LANES, cols_per_iter=2),
        grid=(t // tb, h),
        in_specs=[pl.BlockSpec((tb, 2 * half), lambda j, hh: (j, hh)),
                  pl.BlockSpec((1, 2, nk, half), lambda j, hh: (hh, 0, 0, 0))],
        out_specs=[out_spec, out_spec, out_spec, out_spec],
        out_shape=[out_f32, out_f32, out_f32, out_f32],
        compiler_params=_params(("parallel", "parallel")),
        name="peer_router",
    )(qp, sub_keys)


def _row_to_packed_tile(ref, h, c, row, n_rows):
    rep = jnp.broadcast_to(ref[h, c, pl.ds(row, 1), :], (BF16_SUBLANES, LANES))
    packed = rep.astype(BF16)
    return pltpu.repeat(packed, n_rows // BF16_SUBLANES, axis=0)


def _peer_kernel(x_ref, u_ref, v_ref, cnt_ref, e1_ref, rank2_in, e2_in, o_ref,
                 acc_ref, act_a, act_b, hid_a, hid_b, rank2_ref, e2_ref, *, n_sub, n_e, n_items):
    s = pl.program_id(0)
    vec_e = jnp.clip(s - 1, 0, n_items - 1) % n_e
    out_e = jnp.clip(s - 2, 0, n_items - 1) % n_e
    tb = x_ref.shape[0]
    nk = PEER_KEYS
    n_col = tb // LANES

    def e2_rows(h, c):
        return pl.ds(BF16_SUBLANES + (h * n_col + c) * nk, nk)

    @pl.when(s == 0)
    def _():
        for ref in (act_a, act_b, hid_a, hid_b):
            ref[...] = jnp.zeros_like(ref)

    @pl.when(vec_e == 0)
    def _():
        rank2_ref[...] = rank2_in[...].astype(BF16)
        for h in range(PEER_HEADS):
            for c in range(n_col):
                e2_ref[e2_rows(h, c), :] = e2_in[h, c].astype(BF16)

    @pl.when(out_e == 0)
    def _():
        acc_ref[...] = jnp.zeros_like(acc_ref)

    def gated_tile(act_r, hid_w, ii, c):
        i1 = vec_e * n_sub + ii
        gate = jnp.zeros((nk, LANES), BF16)
        for h in range(PEER_HEADS):
            cnt = _row_to_packed_tile(cnt_ref, h, c, i1, nk)
            e1 = _row_to_packed_tile(e1_ref, h, c, i1, nk)
            gate = gate + jnp.where(rank2_ref[h, c] < cnt, e2_ref[e2_rows(h, c), :] * e1, jnp.zeros((), BF16))
        a = act_r[ii * nk:(ii + 1) * nk, c * LANES:(c + 1) * LANES]
        gelu = 0.5 * a * (1.0 + lax.erf(a * (2.0 ** -0.5)))
        hid_w[c * LANES:(c + 1) * LANES, ii * nk:(ii + 1) * nk] = (gelu.astype(BF16) * gate).T

    def stages(act_w, act_r, hid_w, hid_r):
        half = tb // 2
        for p in range(2):
            act_w[:, p * half:(p + 1) * half] = lax.dot_general(
                u_ref[...], x_ref[p * half:(p + 1) * half, :], NT_DIMS, preferred_element_type=F32)
        tiles = [(ii, c) for ii in range(n_sub) for c in range(n_col)]
        n_chunks = v_ref.shape[1] // V7X_MXU_WIDTH
        per_chunk = -(-len(tiles) // n_chunks)
        for k in range(n_chunks):
            cols = slice(k * V7X_MXU_WIDTH, (k + 1) * V7X_MXU_WIDTH)
            acc_ref[:, cols] += jnp.dot(hid_r[...], v_ref[:, cols], preferred_element_type=F32)
            for ii, c in tiles[k * per_chunk:(k + 1) * per_chunk]:
                gated_tile(act_r, hid_w, ii, c)

    @pl.when(s % 2 == 0)
    def _():
        stages(act_a, act_b, hid_b, hid_a)

    @pl.when(s % 2 == 1)
    def _():
        stages(act_b, act_a, hid_a, hid_b)

    @pl.when((s >= 2) & (out_e == n_e - 1))
    def _():
        o_ref[...] = acc_ref[...]


def _peer_experts(xb, u, v, cnt, e1, rank2, e2, tb, eb):
    t, d = xb.shape
    n_exp = u.shape[0]
    h, _, nk, _ = cnt.shape
    n_e = n_exp // eb
    n_items = (t // tb) * n_e
    assert n_items % 2 == 0
    mm_item = lambda s: jnp.minimum(s, n_items - 1)
    vec_item = lambda s: jnp.clip(s - 1, 0, n_items - 1)
    out_item = lambda s: jnp.clip(s - 2, 0, n_items - 1)
    route = pl.BlockSpec((h, tb // LANES, nk, LANES), lambda s: (0, vec_item(s) // n_e, 0, 0))
    return pl.pallas_call(
        functools.partial(_peer_kernel, n_sub=eb // nk, n_e=n_e, n_items=n_items),
        grid=(n_items + 2,),
        in_specs=[pl.BlockSpec((tb, d), lambda s: (mm_item(s) // n_e, 0)),
                  pl.BlockSpec((eb, d), lambda s: (mm_item(s) % n_e, 0)),
                  pl.BlockSpec((eb, d), lambda s: (out_item(s) % n_e, 0)),
                  route, route, route, route],
        out_specs=pl.BlockSpec((tb, d), lambda s: (out_item(s) // n_e, 0)),
        out_shape=jax.ShapeDtypeStruct((t, d), F32),
        scratch_shapes=[pltpu.VMEM((tb, d), F32),
                        pltpu.VMEM((eb, tb), F32), pltpu.VMEM((eb, tb), F32),
                        pltpu.VMEM((tb, eb), BF16), pltpu.VMEM((tb, eb), BF16),
                        pltpu.VMEM((h, tb // LANES, nk, LANES), BF16),
                        pltpu.VMEM((h * (tb // LANES) * nk + BF16_SUBLANES, LANES), BF16)],
        compiler_params=_params(("arbitrary",)),
        name="peer_experts",
    )(xb, u, v, cnt, e1, rank2, e2)


def _res_ln_kernel(x_ref, f_ref, lw_ref, lb_ref, o_ref, *, alpha):
    o_ref[...] = _layer_norm(alpha * x_ref[...] + f_ref[...], lw_ref[...], lb_ref[...])


def _res_ln(x, f, lw, lb, alpha):
    m, d = x.shape
    bm = 512
    row = pl.BlockSpec((bm, d), lambda i: (i, 0))
    full = pl.BlockSpec((1, d), lambda i: (0, 0))
    return pl.pallas_call(
        functools.partial(_res_ln_kernel, alpha=alpha),
        grid=(m // bm,),
        in_specs=[row, row, full, full],
        out_specs=row,
        out_shape=jax.ShapeDtypeStruct((m, d), F32),
        compiler_params=_params(("parallel",)),
        name="res_ln",
    )(x, f, lw.reshape(1, d), lb.reshape(1, d))


def _layer(x, w_in, sb_norm_w, ret_norm_w, w_out, ln1_w, ln1_b, w_pq, sub_keys, expert_u, expert_v,
           ln2_w, ln2_b, alpha):
    b, s, d = x.shape
    n_tok = b * s
    sb_heads = sb_norm_w.shape[0] // HEAD_DIM
    ret_heads = ret_norm_w.shape[0] // HEAD_DIM
    xt = x.reshape(n_tok, d)
    col_scale = jnp.where(jnp.arange(w_in.shape[1]) < sb_heads * HEAD_DIM, SB_Q_SCALE, 1.0).astype(F32)
    proj = _matmul(xt.astype(BF16), (w_in * col_scale).astype(BF16), 1024, 1024, BF16).reshape(b, s, -1)
    sb = _sb_attention(proj, sb_norm_w, sb_heads, 0)
    ret = _retention(proj, ret_norm_w, ret_heads, 3 * sb_heads)
    x1, x1b = _outproj_ln(sb.reshape(n_tok, -1), ret.reshape(n_tok, -1), w_out.astype(BF16), xt,
                          ln1_w, ln1_b, alpha)
    qp = _matmul(x1b, w_pq.astype(BF16), 1024, 1024, BF16)
    cnt, e1, rank2, e2 = _router(qp, sub_keys.astype(BF16), 512)
    ffn = _peer_experts(x1b, expert_u.astype(BF16), expert_v.astype(BF16), cnt, e1, rank2, e2, 512, 512)
    return _res_ln(x1, ffn, ln2_w, ln2_b, alpha).reshape(b, s, d)


def kernel(x, w_in, sb_norm_w, ret_norm_w, w_out, ln1_w, ln1_b, w_pq, sub_keys, expert_u, expert_v, ln2_w, ln2_b):
    depth = w_in.shape[0]
    alpha = (2.0 * depth) ** 0.25
    for layer in range(depth):
        x = _layer(x, w_in[layer], sb_norm_w[layer], ret_norm_w[layer], w_out[layer], ln1_w[layer],
                   ln1_b[layer], w_pq[layer], sub_keys[layer], expert_u[layer], expert_v[layer],
                   ln2_w[layer], ln2_b[layer], alpha)
    return x
```

```python
import functools
import math

import jax
import jax.numpy as jnp
from jax import lax
from jax.experimental import pallas as pl
from jax.experimental.pallas import tpu as pltpu

F32 = jnp.float32
BF16 = jnp.bfloat16

LANES = 128
F32_SUBLANES = 8
BF16_SUBLANES = 16
HEAD_DIM = 128
CHUNK = 64
ROPE_THETA = 10000.0
PEER_HEADS = 8
PEER_KEYS = 128
PEER_TOPK = 16
LN_EPS = 1e-5
NORM_EPS = 1e-6
SB_Q_SCALE = HEAD_DIM ** -0.5 * math.log2(math.e)

V7X_VMEM_BYTES = 64 * 1024 * 1024
V7X_MXU_WIDTH = 256
VMEM_LIMIT = 52 * 1024 * 1024

NT_DIMS = (((1,), (1,)), ((), ()))
TN_DIMS = (((0,), (0,)), ((), ()))


def _params(semantics):
    return pltpu.CompilerParams(dimension_semantics=semantics, vmem_limit_bytes=VMEM_LIMIT)


def _matmul_kernel(x_ref, w_ref, o_ref):
    o_ref[...] = jnp.dot(x_ref[...], w_ref[...], preferred_element_type=F32).astype(o_ref.dtype)


def _matmul(x, w, bm, bn, out_dtype):
    m, k = x.shape
    n = w.shape[1]
    return pl.pallas_call(
        _matmul_kernel,
        grid=(n // bn, m // bm),
        in_specs=[pl.BlockSpec((bm, k), lambda j, i: (i, 0)),
                  pl.BlockSpec((k, bn), lambda j, i: (0, j))],
        out_specs=pl.BlockSpec((bm, bn), lambda j, i: (i, j)),
        out_shape=jax.ShapeDtypeStruct((m, n), out_dtype),
        compiler_params=_params(("parallel", "parallel")),
        name="dense_proj",
    )(x, w)


def _sb_kernel(q_ref, k_ref, v_ref, tri_ref, w_ref, o_ref, acc_ref, run_ref, *, t, n_group):
    qi = pl.program_id(2)
    tri = tri_ref[...]
    below_diag = (lax.broadcasted_iota(jnp.int32, (t, t), 1) < lax.broadcasted_iota(jnp.int32, (t, t), 0))
    lanes = [slice(g * HEAD_DIM, (g + 1) * HEAD_DIM) for g in range(n_group)]

    def block(j, on_diagonal):
        start = pl.multiple_of(j * t, t)

        def scores(g):
            return lax.dot_general(q_ref[0, :, lanes[g]], k_ref[0, pl.ds(start, t), lanes[g]], NT_DIMS,
                                   preferred_element_type=F32)

        def suffix_sums(z):
            softplus = jnp.maximum(z, 0.0) + jnp.log2(1.0 + jnp.exp2(-jnp.abs(z)))
            log_beta = z - softplus
            if on_diagonal:
                softplus = jnp.where(below_diag, softplus, 0.0)
            hi = softplus.astype(BF16)
            lo = (softplus - hi.astype(F32)).astype(BF16)
            suffix = jnp.dot(jnp.concatenate([hi, lo], axis=1), tri, preferred_element_type=F32)
            return log_beta, suffix, jnp.sum(softplus, axis=1, keepdims=True)

        def weighted_values(g, log_beta, suffix, total):
            vs = v_ref[0, pl.ds(start, t), lanes[g]]
            if on_diagonal:
                a = jnp.where(below_diag, jnp.exp2(log_beta - suffix), 0.0)
                acc_ref[:, lanes[g]] = jnp.dot(a.astype(BF16), vs, preferred_element_type=F32)
                run_ref[g] = total
            else:
                run = run_ref[g]
                a = jnp.exp2(log_beta - suffix - run)
                acc_ref[:, lanes[g]] += jnp.dot(a.astype(BF16), vs, preferred_element_type=F32)
                run_ref[g] = run + total

        z, mid = {}, {}
        for step in range(n_group + 2):
            if step < n_group:
                z[step] = scores(step)
            if 0 <= step - 1 < n_group:
                mid[step - 1] = suffix_sums(z.pop(step - 1))
            if 0 <= step - 2 < n_group:
                weighted_values(step - 2, *mid.pop(step - 2))

    block(qi, True)

    def off_diagonal(jj, _):
        block(qi - jj, False)
        return 0

    lax.fori_loop(1, qi + 1, off_diagonal, 0)
    for g in range(n_group):
        acc = acc_ref[:, lanes[g]]
        ms = jnp.mean(acc * acc, axis=-1, keepdims=True)
        o_ref[0, :, lanes[g]] = (acc * lax.rsqrt(ms + NORM_EPS) * w_ref[:, lanes[g]]).astype(o_ref.dtype)


def _sb_attention(proj, norm_w, n_heads, col0, n_group=8, t=256):
    b, s, _ = proj.shape
    width = n_group * HEAD_DIM
    strict_lower = (jnp.arange(t)[:, None] > jnp.arange(t)[None, :]).astype(BF16)
    tri = jnp.concatenate([strict_lower, strict_lower], axis=0)
    kern = functools.partial(_sb_kernel, t=t, n_group=n_group)
    group_col = lambda part: (col0 + part * n_heads) // n_group
    return pl.pallas_call(
        kern,
        grid=(b, n_heads // n_group, s // t),
        in_specs=[pl.BlockSpec((1, t, width), lambda bi, hg, i: (bi, i, group_col(0) + hg)),
                  pl.BlockSpec((1, s, width), lambda bi, hg, i: (bi, 0, group_col(1) + hg)),
                  pl.BlockSpec((1, s, width), lambda bi, hg, i: (bi, 0, group_col(2) + hg)),
                  pl.BlockSpec((2 * t, t), lambda bi, hg, i: (0, 0)),
                  pl.BlockSpec((1, width), lambda bi, hg, i: (0, hg))],
        out_specs=pl.BlockSpec((1, t, width), lambda bi, hg, i: (bi, i, hg)),
        out_shape=jax.ShapeDtypeStruct((b, s, n_heads * HEAD_DIM), BF16),
        scratch_shapes=[pltpu.VMEM((t, width), F32), pltpu.VMEM((n_group, t, 1), F32)],
        compiler_params=_params(("parallel", "parallel", "arbitrary")),
        name="sb_attention",
    )(proj, proj, proj, tri, norm_w.reshape(1, -1).astype(F32))


def _ret_kernel(q_ref, k_ref, v_ref, g_ref, cos_ref, sin_ref, dec_ref, xi_ref, zeta_ref, gt_ref, w_ref,
                o_ref, state_ref):
    @pl.when(pl.program_id(2) == 0)
    def _():
        state_ref[...] = jnp.zeros_like(state_ref)

    cos = cos_ref[...]
    sin = sin_ref[...]
    q = q_ref[0].astype(F32)
    k = k_ref[0].astype(F32)
    half = HEAD_DIM // 2
    qr = q * cos + pltpu.roll(q, half, 1) * sin
    kr = k * cos + pltpu.roll(k, half, 1) * sin
    qb = qr.astype(BF16)
    v = v_ref[0]
    state = state_ref[...]
    scores = lax.dot_general(qb, kr.astype(BF16), NT_DIMS, preferred_element_type=F32) * dec_ref[0]
    ret = jnp.dot(scores.astype(BF16), v, preferred_element_type=F32)
    ret = ret + jnp.dot(qb, state.astype(BF16), preferred_element_type=F32) * xi_ref[0]
    kz = (kr * zeta_ref[0]).astype(BF16)
    state_ref[...] = state * gt_ref[0] + lax.dot_general(kz, v, TN_DIMS, preferred_element_type=F32)
    mu = jnp.mean(ret, axis=-1, keepdims=True)
    cen = ret - mu
    var = jnp.mean(cen * cen, axis=-1, keepdims=True)
    g = g_ref[0].astype(F32)
    silu = g / (1.0 + jnp.exp(-g))
    o_ref[0] = (cen * lax.rsqrt(var + NORM_EPS) * w_ref[...] * silu).astype(o_ref.dtype)


def _retention(proj, norm_w, n_heads, col0):
    b, s, _ = proj.shape
    t = 256
    d = HEAD_DIM
    inv_freq = ROPE_THETA ** (-jnp.arange(0, d, 2, dtype=F32) / d)
    ang = jnp.arange(s, dtype=F32)[:, None] * inv_freq[None, :]
    ang = jnp.concatenate([ang, ang], -1)
    sign = jnp.concatenate([-jnp.ones((d // 2,), F32), jnp.ones((d // 2,), F32)])
    cos_t = jnp.cos(ang)
    sin_t = jnp.sin(ang) * sign[None, :]
    log_g = jnp.log1p(-(2.0 ** (-5.0 - jnp.arange(n_heads, dtype=F32))))
    idx = jnp.arange(t, dtype=F32)
    same_or_earlier_chunk = (jnp.floor(idx[None, :] / CHUNK) <= jnp.floor(idx[:, None] / CHUNK))
    dec = jnp.exp(log_g[:, None, None] * jnp.abs(idx[:, None] - idx[None, :])) * (d ** -0.5)
    dec = jnp.where(same_or_earlier_chunk[None], dec, 0.0)
    xi = jnp.broadcast_to(jnp.exp(log_g[:, None] * (idx + 1.0))[:, :, None], (n_heads, t, d))
    zeta = jnp.broadcast_to((jnp.exp(log_g[:, None] * (t - 1.0 - idx)) * (d ** -0.5))[:, :, None], (n_heads, t, d))
    gt = jnp.broadcast_to(jnp.exp(log_g * t)[:, None, None], (n_heads, 1, d))
    head_spec = lambda off: pl.BlockSpec((1, t, d), lambda bi, h, i: (bi, i, col0 + off * n_heads + h))
    return pl.pallas_call(
        _ret_kernel,
        grid=(b, n_heads, s // t),
        in_specs=[head_spec(0), head_spec(1), head_spec(2), head_spec(3),
                  pl.BlockSpec((t, d), lambda bi, h, i: (i, 0)),
                  pl.BlockSpec((t, d), lambda bi, h, i: (i, 0)),
                  pl.BlockSpec((1, t, t), lambda bi, h, i: (h, 0, 0)),
                  pl.BlockSpec((1, t, d), lambda bi, h, i: (h, 0, 0)),
                  pl.BlockSpec((1, t, d), lambda bi, h, i: (h, 0, 0)),
                  pl.BlockSpec((1, 1, d), lambda bi, h, i: (h, 0, 0)),
                  pl.BlockSpec((1, d), lambda bi, h, i: (0, h))],
        out_specs=pl.BlockSpec((1, t, d), lambda bi, h, i: (bi, i, h)),
        out_shape=jax.ShapeDtypeStruct((b, s, n_heads * d), BF16),
        scratch_shapes=[pltpu.VMEM((d, d), F32)],
        compiler_params=_params(("parallel", "parallel", "arbitrary")),
        name="retention",
    )(proj, proj, proj, proj, cos_t, sin_t, dec, xi, zeta, gt, norm_w.reshape(1, -1).astype(F32))


def _layer_norm(y, w, b):
    mu = jnp.mean(y, axis=-1, keepdims=True)
    cen = y - mu
    var = jnp.mean(cen * cen, axis=-1, keepdims=True)
    return cen * lax.rsqrt(var + LN_EPS) * w + b


def _outproj_ln_kernel(sb_ref, ret_ref, wo_ref, x_ref, lw_ref, lb_ref, o_ref, ob_ref, *, alpha, split):
    mix = jnp.dot(sb_ref[...], wo_ref[:split, :], preferred_element_type=F32)
    mix = mix + jnp.dot(ret_ref[...], wo_ref[split:, :], preferred_element_type=F32)
    y = _layer_norm(alpha * x_ref[...] + mix, lw_ref[...], lb_ref[...])
    o_ref[...] = y
    ob_ref[...] = y.astype(BF16)


def _outproj_ln(sb, ret, wo, x, lw, lb, alpha):
    m, d = x.shape
    split = sb.shape[1]
    bm = 256
    kern = functools.partial(_outproj_ln_kernel, alpha=alpha, split=split)
    row = lambda width: pl.BlockSpec((bm, width), lambda i: (i, 0))
    full = lambda shape: pl.BlockSpec(shape, lambda i: (0, 0))
    return pl.pallas_call(
        kern,
        grid=(m // bm,),
        in_specs=[row(split), row(ret.shape[1]), full(wo.shape), row(d), full((1, d)), full((1, d))],
        out_specs=[row(d), row(d)],
        out_shape=[jax.ShapeDtypeStruct((m, d), F32), jax.ShapeDtypeStruct((m, d), BF16)],
        compiler_params=_params(("parallel",)),
        name="outproj_ln",
    )(sb, ret, wo, x, lw.reshape(1, d), lb.reshape(1, d))


def _top_ranks(s, iota, k, exact_ties):
    n = s.shape[0]
    work = s
    rank = jnp.full(s.shape, float(k), F32)
    tops = []
    for r in range(k):
        m = jnp.max(work, axis=0, keepdims=True)
        sel = work == m
        if exact_ties:
            first = jnp.min(jnp.where(sel, iota, float(n)), axis=0, keepdims=True)
            sel = iota == first
        rank = jnp.where(sel, float(r), rank)
        work = jnp.where(sel, -jnp.inf, work)
        tops.append(m)
    return rank, tops


def _route_column(s1, s2, iota, ciota, widths, exact_ties):
    k = PEER_TOPK
    nk, lanes = s1.shape
    n_cand = sum(widths)
    n_pad = ciota.shape[0]
    rank1, top1 = _top_ranks(s1, iota, k, exact_ties)
    rank2, top2 = _top_ranks(s2, iota, k, exact_ties)
    top2_all = jnp.concatenate(top2, axis=0)
    cand = [top1[a] + top2_all[:widths[a]] for a in range(k)]
    if n_pad > n_cand:
        cand.append(jnp.full((n_pad - n_cand, lanes), -jnp.inf, F32))
    cand = jnp.concatenate(cand, axis=0)
    crank, _ = _top_ranks(cand, ciota, k, exact_ties)
    chosen = crank < float(k)
    best = top1[0] + top2[0]
    z = jnp.sum(jnp.where(chosen, jnp.exp(cand - best), 0.0), axis=0, keepdims=True)
    ones = jnp.where(chosen, 1.0, 0.0)
    cnt = jnp.zeros((nk, lanes), F32)
    off = 0
    for a in range(k):
        cnt_a = jnp.sum(ones[off:off + widths[a]], axis=0, keepdims=True)
        cnt = jnp.where(rank1 == float(a), cnt_a, cnt)
        off += widths[a]
    picked = (jnp.sum(jnp.where(rank1 < float(k), 1.0, 0.0), axis=0, keepdims=True)
              + jnp.sum(jnp.where(rank2 < float(k), 1.0, 0.0), axis=0, keepdims=True)
              + jnp.sum(ones, axis=0, keepdims=True))
    unique = picked == float(3 * k)
    return (cnt, jnp.exp(s1 - top1[0]), rank2, jnp.exp(s2 - top2[0]) / z), unique


def _router_kernel(q_ref, keys_ref, cnt_ref, e1_ref, rank2_ref, e2_ref, *, lanes, cols_per_iter):
    nk = PEER_KEYS
    k = PEER_TOPK
    iota = lax.broadcasted_iota(jnp.int32, (nk, lanes), 0).astype(F32)
    widths = [k // (a + 1) for a in range(k)]
    n_pad = -(-sum(widths) // F32_SUBLANES) * F32_SUBLANES
    ciota = lax.broadcasted_iota(jnp.int32, (n_pad, lanes), 0).astype(F32)
    out_refs = (cnt_ref, e1_ref, rank2_ref, e2_ref)

    def columns(group, _):
        pending = []
        for j in range(cols_per_iter):
            c = group * cols_per_iter + j
            sl = pl.ds(pl.multiple_of(c * lanes, lanes), lanes)
            q = q_ref[sl, :]
            half = q.shape[1] // 2
            s1 = lax.dot_general(keys_ref[0, 0], q[:, :half], NT_DIMS, preferred_element_type=F32)
            s2 = lax.dot_general(keys_ref[0, 1], q[:, half:], NT_DIMS, preferred_element_type=F32)
            outs, unique = _route_column(s1, s2, iota, ciota, widths, exact_ties=False)
            for ref, val in zip(out_refs, outs):
                ref[0, c] = val
            pending.append((c, s1, s2, jnp.min(jnp.where(unique, 1.0, 0.0))))
        for c, s1, s2, all_unique in pending:
            @pl.when(all_unique < 0.5)
            def _():
                exact, _ = _route_column(s1, s2, iota, ciota, widths, exact_ties=True)
                for ref, val in zip(out_refs, exact):
                    ref[0, c] = val
        return 0

    lax.fori_loop(0, q_ref.shape[0] // (lanes * cols_per_iter), columns, 0)


def _router(qp, sub_keys, tb):
    t = qp.shape[0]
    h, _, nk, half = sub_keys.shape
    out_f32 = jax.ShapeDtypeStruct((h, t // LANES, nk, LANES), F32)
    out_spec = pl.BlockSpec((1, tb // LANES, nk, LANES), lambda j, hh: (hh, j, 0, 0))
    return pl.pallas_call(
        functools.partial(_router_kernel, lanes=LANES, cols_per_iter=2),
        grid=(t // tb, h),
        in_specs=[pl.BlockSpec((tb, 2 * half), lambda j, hh: (j, hh)),
                  pl.BlockSpec((1, 2, nk, half), lambda j, hh: (hh, 0, 0, 0))],
        out_specs=[out_spec, out_spec, out_spec, out_spec],
        out_shape=[out_f32, out_f32, out_f32, out_f32],
        compiler_params=_params(("parallel", "parallel")),
        name="peer_router",
    )(qp, sub_keys)


def _row_to_packed_tile(ref, h, c, row, n_rows):
    rep = jnp.broadcast_to(ref[h, c, 0, row:row + 1, :], (BF16_SUBLANES, LANES))
    packed = rep.astype(BF16)
    return pltpu.repeat(packed, n_rows // BF16_SUBLANES, axis=0)


def _peer_kernel(x_ref, u_ref, v_ref, cnt_ref, e1_ref, rank2_in, e2_in, o_ref,
                 acc_ref, act_a, act_b, hid_a, hid_b, rank2_ref, e2_ref, *, n_sub, n_e, n_items):
    s = pl.program_id(0)
    vec_e = jnp.clip(s - 1, 0, n_items - 1) % n_e
    out_e = jnp.clip(s - 2, 0, n_items - 1) % n_e
    tb = x_ref.shape[0]
    nk = PEER_KEYS
    n_col = tb // LANES

    def e2_rows(h, c):
        return pl.ds(BF16_SUBLANES + (h * n_col + c) * nk, nk)

    @pl.when(s == 0)
    def _():
        for ref in (act_a, act_b, hid_a, hid_b):
            ref[...] = jnp.zeros_like(ref)

    @pl.when(vec_e == 0)
    def _():
        rank2_ref[...] = rank2_in[...].astype(BF16)
        for h in range(PEER_HEADS):
            for c in range(n_col):
                e2_ref[e2_rows(h, c), :] = e2_in[h, c].astype(BF16)

    @pl.when(out_e == 0)
    def _():
        acc_ref[...] = jnp.zeros_like(acc_ref)

    def gated_tile(act_r, hid_w, ii, c):
        gate = jnp.zeros((nk, LANES), BF16)
        for h in range(PEER_HEADS):
            cnt = _row_to_packed_tile(cnt_ref, h, c, ii, nk)
            e1 = _row_to_packed_tile(e1_ref, h, c, ii, nk)
            gate = gate + jnp.where(rank2_ref[h, c] < cnt, e2_ref[e2_rows(h, c), :] * e1, jnp.zeros((), BF16))
        a = act_r[ii * nk:(ii + 1) * nk, c * LANES:(c + 1) * LANES]
        gelu = 0.5 * a * (1.0 + lax.erf(a * (2.0 ** -0.5)))
        hid_w[c * LANES:(c + 1) * LANES, ii * nk:(ii + 1) * nk] = (gelu.astype(BF16) * gate).T

    def stages(act_w, act_r, hid_w, hid_r):
        half = tb // 2
        for p in range(2):
            act_w[:, p * half:(p + 1) * half] = lax.dot_general(
                u_ref[...], x_ref[p * half:(p + 1) * half, :], NT_DIMS, preferred_element_type=F32)
        tiles = [(ii, c) for ii in range(n_sub) for c in range(n_col)]
        n_chunks = v_ref.shape[1] // V7X_MXU_WIDTH
        per_chunk = -(-len(tiles) // n_chunks)
        for k in range(n_chunks):
            cols = slice(k * V7X_MXU_WIDTH, (k + 1) * V7X_MXU_WIDTH)
            acc_ref[:, cols] += jnp.dot(hid_r[...], v_ref[:, cols], preferred_element_type=F32)
            for ii, c in tiles[k * per_chunk:(k + 1) * per_chunk]:
                gated_tile(act_r, hid_w, ii, c)

    @pl.when(s % 2 == 0)
    def _():
        stages(act_a, act_b, hid_b, hid_a)

    @pl.when(s % 2 == 1)
    def _():
        stages(act_b, act_a, hid_a, hid_b)

    @pl.when((s >= 2) & (out_e == n_e - 1))
    def _():
        o_ref[...] = acc_ref[...]


def _peer_experts(xb, u, v, cnt, e1, rank2, e2, tb, eb):
    t, d = xb.shape
    n_exp = u.shape[0]
    h, n_col_all, nk, _ = cnt.shape
    assert eb == F32_SUBLANES * nk
    n_e = n_exp // eb
    n_items = (t // tb) * n_e
    assert n_items % 2 == 0
    mm_item = lambda s: jnp.minimum(s, n_items - 1)
    vec_item = lambda s: jnp.clip(s - 1, 0, n_items - 1)
    out_item = lambda s: jnp.clip(s - 2, 0, n_items - 1)
    route = pl.BlockSpec((h, tb // LANES, nk, LANES), lambda s: (0, vec_item(s) // n_e, 0, 0))
    first_key_rows = pl.BlockSpec((h, tb // LANES, 1, F32_SUBLANES, LANES),
                                  lambda s: (0, vec_item(s) // n_e, vec_item(s) % n_e, 0, 0))
    cnt, e1 = (a.reshape(h, n_col_all, nk // F32_SUBLANES, F32_SUBLANES, LANES) for a in (cnt, e1))
    return pl.pallas_call(
        functools.partial(_peer_kernel, n_sub=eb // nk, n_e=n_e, n_items=n_items),
        grid=(n_items + 2,),
        in_specs=[pl.BlockSpec((tb, d), lambda s: (mm_item(s) // n_e, 0)),
                  pl.BlockSpec((eb, d), lambda s: (mm_item(s) % n_e, 0)),
                  pl.BlockSpec((eb, d), lambda s: (out_item(s) % n_e, 0)),
                  first_key_rows, first_key_rows, route, route],
        out_specs=pl.BlockSpec((tb, d), lambda s: (out_item(s) // n_e, 0)),
        out_shape=jax.ShapeDtypeStruct((t, d), F32),
        scratch_shapes=[pltpu.VMEM((tb, d), F32),
                        pltpu.VMEM((eb, tb), F32), pltpu.VMEM((eb, tb), F32),
                        pltpu.VMEM((tb, eb), BF16), pltpu.VMEM((tb, eb), BF16),
                        pltpu.VMEM((h, tb // LANES, nk, LANES), BF16),
                        pltpu.VMEM((h * (tb // LANES) * nk + BF16_SUBLANES, LANES), BF16)],
        compiler_params=_params(("arbitrary",)),
        name="peer_experts",
    )(xb, u, v, cnt, e1, rank2, e2)


def _res_ln_kernel(x_ref, f_ref, lw_ref, lb_ref, o_ref, *, alpha):
    o_ref[...] = _layer_norm(alpha * x_ref[...] + f_ref[...], lw_ref[...], lb_ref[...])


def _res_ln(x, f, lw, lb, alpha):
    m, d = x.shape
    bm = 512
    row = pl.BlockSpec((bm, d), lambda i: (i, 0))
    full = pl.BlockSpec((1, d), lambda i: (0, 0))
    return pl.pallas_call(
        functools.partial(_res_ln_kernel, alpha=alpha),
        grid=(m // bm,),
        in_specs=[row, row, full, full],
        out_specs=row,
        out_shape=jax.ShapeDtypeStruct((m, d), F32),
        compiler_params=_params(("parallel",)),
        name="res_ln",
    )(x, f, lw.reshape(1, d), lb.reshape(1, d))


def _layer(x, w_in, sb_norm_w, ret_norm_w, w_out, ln1_w, ln1_b, w_pq, sub_keys, expert_u, expert_v,
           ln2_w, ln2_b, alpha):
    b, s, d = x.shape
    n_tok = b * s
    sb_heads = sb_norm_w.shape[0] // HEAD_DIM
    ret_heads = ret_norm_w.shape[0] // HEAD_DIM
    xt = x.reshape(n_tok, d)
    col_scale = jnp.where(jnp.arange(w_in.shape[1]) < sb_heads * HEAD_DIM, SB_Q_SCALE, 1.0).astype(F32)
    proj = _matmul(xt.astype(BF16), (w_in * col_scale).astype(BF16), 1024, 1024, BF16).reshape(b, s, -1)
    sb = _sb_attention(proj, sb_norm_w, sb_heads, 0)
    ret = _retention(proj, ret_norm_w, ret_heads, 3 * sb_heads)
    x1, x1b = _outproj_ln(sb.reshape(n_tok, -1), ret.reshape(n_tok, -1), w_out.astype(BF16), xt,
                          ln1_w, ln1_b, alpha)
    qp = _matmul(x1b, w_pq.astype(BF16), 1024, 1024, BF16)
    cnt, e1, rank2, e2 = _router(qp, sub_keys.astype(BF16), 512)
    ffn = _peer_experts(x1b, expert_u.astype(BF16), expert_v.astype(BF16), cnt, e1, rank2, e2, 512, 1024)
    return _res_ln(x1, ffn, ln2_w, ln2_b, alpha).reshape(b, s, d)


def kernel(x, w_in, sb_norm_w, ret_norm_w, w_out, ln1_w, ln1_b, w_pq, sub_keys, expert_u, expert_v, ln2_w, ln2_b):
    depth = w_in.shape[0]
    alpha = (2.0 * depth) ** 0.25
    for layer in range(depth):
        x = _layer(x, w_in[layer], sb_norm_w[layer], ret_norm_w[layer], w_out[layer], ln1_w[layer],
                   ln1_b[layer], w_pq[layer], sub_keys[layer], expert_u[layer], expert_v[layer],
                   ln2_w[layer], ln2_b[layer], alpha)
    return x
```

```python
import functools
import math

import jax
import jax.numpy as jnp
from jax import lax
from jax.experimental import pallas as pl
from jax.experimental.pallas import tpu as pltpu

F32 = jnp.float32
BF16 = jnp.bfloat16

LANES = 128
F32_SUBLANES = 8
BF16_SUBLANES = 16
HEAD_DIM = 128
CHUNK = 64
ROPE_THETA = 10000.0
PEER_HEADS = 8
PEER_KEYS = 128
PEER_TOPK = 16
LN_EPS = 1e-5
NORM_EPS = 1e-6
SB_Q_SCALE = HEAD_DIM ** -0.5 * math.log2(math.e)
F32_EXP2_UNDERFLOW = 160.0

V7X_VMEM_BYTES = 64 * 1024 * 1024
V7X_MXU_WIDTH = 256
VMEM_LIMIT = 52 * 1024 * 1024

NT_DIMS = (((1,), (1,)), ((), ()))
TN_DIMS = (((0,), (0,)), ((), ()))


def _params(semantics):
    return pltpu.CompilerParams(dimension_semantics=semantics, vmem_limit_bytes=VMEM_LIMIT)


def _matmul_kernel(x_ref, w_ref, o_ref):
    o_ref[...] = jnp.dot(x_ref[...], w_ref[...], preferred_element_type=F32).astype(o_ref.dtype)


def _matmul(x, w, bm, bn, out_dtype):
    m, k = x.shape
    n = w.shape[1]
    return pl.pallas_call(
        _matmul_kernel,
        grid=(n // bn, m // bm),
        in_specs=[pl.BlockSpec((bm, k), lambda j, i: (i, 0)),
                  pl.BlockSpec((k, bn), lambda j, i: (0, j))],
        out_specs=pl.BlockSpec((bm, bn), lambda j, i: (i, j)),
        out_shape=jax.ShapeDtypeStruct((m, n), out_dtype),
        compiler_params=_params(("parallel", "parallel")),
        name="dense_proj",
    )(x, w)


def _sb_kernel(q_ref, k_ref, v_ref, tri_ref, w_ref, o_ref, acc_ref, run_ref, *, t, n_group):
    qi = pl.program_id(2)
    tri = tri_ref[...]
    below_diag = (lax.broadcasted_iota(jnp.int32, (t, t), 1) < lax.broadcasted_iota(jnp.int32, (t, t), 0))
    lanes = [slice(g * HEAD_DIM, (g + 1) * HEAD_DIM) for g in range(n_group)]

    def block(j, on_diagonal):
        start = pl.multiple_of(j * t, t)

        def scores(g):
            return lax.dot_general(q_ref[0, :, lanes[g]], k_ref[0, pl.ds(start, t), lanes[g]], NT_DIMS,
                                   preferred_element_type=F32)

        def suffix_sums(z):
            softplus = jnp.maximum(z, 0.0) + jnp.log2(1.0 + jnp.exp2(-jnp.abs(z)))
            log_beta = z - softplus
            if on_diagonal:
                softplus = jnp.where(below_diag, softplus, 0.0)
            hi = softplus.astype(BF16)
            lo = (softplus - hi.astype(F32)).astype(BF16)
            suffix = jnp.dot(jnp.concatenate([hi, lo], axis=1), tri, preferred_element_type=F32)
            return log_beta, suffix, jnp.sum(softplus, axis=1, keepdims=True)

        def weighted_values(g, log_beta, suffix, total):
            vs = v_ref[0, pl.ds(start, t), lanes[g]]
            if on_diagonal:
                a = jnp.where(below_diag, jnp.exp2(log_beta - suffix), 0.0)
                acc_ref[:, lanes[g]] = jnp.dot(a.astype(BF16), vs, preferred_element_type=F32)
                run_ref[g] = total
            else:
                run = run_ref[g]
                a = jnp.exp2(log_beta - suffix - run)
                acc_ref[:, lanes[g]] += jnp.dot(a.astype(BF16), vs, preferred_element_type=F32)
                run_ref[g] = run + total

        z, mid = {}, {}
        for step in range(n_group + 2):
            if step < n_group:
                z[step] = scores(step)
            if 0 <= step - 1 < n_group:
                mid[step - 1] = suffix_sums(z.pop(step - 1))
            if 0 <= step - 2 < n_group:
                weighted_values(step - 2, *mid.pop(step - 2))

    def smallest_run():
        m = run_ref[0]
        for g in range(1, n_group):
            m = jnp.minimum(m, run_ref[g])
        return jnp.min(m)

    block(qi, True)

    def more_to_do(state):
        jj, low = state
        return jnp.logical_and(jj <= qi, low < F32_EXP2_UNDERFLOW)

    def off_diagonal(state):
        jj, _ = state
        block(qi - jj, False)
        return jj + 1, smallest_run()

    lax.while_loop(more_to_do, off_diagonal, (jnp.int32(1), smallest_run()))
    for g in range(n_group):
        acc = acc_ref[:, lanes[g]]
        ms = jnp.mean(acc * acc, axis=-1, keepdims=True)
        o_ref[0, :, lanes[g]] = (acc * lax.rsqrt(ms + NORM_EPS) * w_ref[:, lanes[g]]).astype(o_ref.dtype)


def _sb_attention(proj, norm_w, n_heads, col0, n_group=8, t=256):
    b, s, _ = proj.shape
    width = n_group * HEAD_DIM
    strict_lower = (jnp.arange(t)[:, None] > jnp.arange(t)[None, :]).astype(BF16)
    tri = jnp.concatenate([strict_lower, strict_lower], axis=0)
    kern = functools.partial(_sb_kernel, t=t, n_group=n_group)
    group_col = lambda part: (col0 + part * n_heads) // n_group
    return pl.pallas_call(
        kern,
        grid=(b, n_heads // n_group, s // t),
        in_specs=[pl.BlockSpec((1, t, width), lambda bi, hg, i: (bi, i, group_col(0) + hg)),
                  pl.BlockSpec((1, s, width), lambda bi, hg, i: (bi, 0, group_col(1) + hg)),
                  pl.BlockSpec((1, s, width), lambda bi, hg, i: (bi, 0, group_col(2) + hg)),
                  pl.BlockSpec((2 * t, t), lambda bi, hg, i: (0, 0)),
                  pl.BlockSpec((1, width), lambda bi, hg, i: (0, hg))],
        out_specs=pl.BlockSpec((1, t, width), lambda bi, hg, i: (bi, i, hg)),
        out_shape=jax.ShapeDtypeStruct((b, s, n_heads * HEAD_DIM), BF16),
        scratch_shapes=[pltpu.VMEM((t, width), F32), pltpu.VMEM((n_group, t, 1), F32)],
        compiler_params=_params(("parallel", "parallel", "arbitrary")),
        name="sb_attention",
    )(proj, proj, proj, tri, norm_w.reshape(1, -1).astype(F32))


def _ret_kernel(q_ref, k_ref, v_ref, g_ref, cos_ref, sin_ref, dec_ref, xi_ref, zeta_ref, gt_ref, w_ref,
                o_ref, state_ref, *, n_group):
    @pl.when(pl.program_id(2) == 0)
    def _():
        state_ref[...] = jnp.zeros_like(state_ref)

    cos = cos_ref[...]
    sin = sin_ref[...]
    half = HEAD_DIM // 2
    lanes = [slice(g * HEAD_DIM, (g + 1) * HEAD_DIM) for g in range(n_group)]

    def rotate_and_score(g):
        q = q_ref[0, :, lanes[g]].astype(F32)
        k = k_ref[0, :, lanes[g]].astype(F32)
        qr = q * cos + pltpu.roll(q, half, 1) * sin
        kr = k * cos + pltpu.roll(k, half, 1) * sin
        qb = qr.astype(BF16)
        scores = lax.dot_general(qb, kr.astype(BF16), NT_DIMS, preferred_element_type=F32)
        carried = jnp.dot(qb, state_ref[g].astype(BF16), preferred_element_type=F32)
        return kr, scores, carried

    def mix_values(g, kr, scores, carried):
        v = v_ref[0, :, lanes[g]]
        ret = jnp.dot((scores * dec_ref[g]).astype(BF16), v, preferred_element_type=F32) + carried * xi_ref[g]
        kz = (kr * zeta_ref[g]).astype(BF16)
        state_ref[g] = state_ref[g] * gt_ref[g] + lax.dot_general(kz, v, TN_DIMS, preferred_element_type=F32)
        return ret

    def normalise(g, ret):
        mu = jnp.mean(ret, axis=-1, keepdims=True)
        cen = ret - mu
        var = jnp.mean(cen * cen, axis=-1, keepdims=True)
        gate = g_ref[0, :, lanes[g]].astype(F32)
        silu = gate / (1.0 + jnp.exp(-gate))
        o_ref[0, :, lanes[g]] = (cen * lax.rsqrt(var + NORM_EPS) * w_ref[:, lanes[g]] * silu).astype(o_ref.dtype)

    first, second = {}, {}
    for step in range(n_group + 2):
        if step < n_group:
            first[step] = rotate_and_score(step)
        if 0 <= step - 1 < n_group:
            second[step - 1] = mix_values(step - 1, *first.pop(step - 1))
        if 0 <= step - 2 < n_group:
            normalise(step - 2, second.pop(step - 2))


def _retention(proj, norm_w, n_heads, col0, n_group=4):
    b, s, _ = proj.shape
    t = 256
    d = HEAD_DIM
    inv_freq = ROPE_THETA ** (-jnp.arange(0, d, 2, dtype=F32) / d)
    ang = jnp.arange(s, dtype=F32)[:, None] * inv_freq[None, :]
    ang = jnp.concatenate([ang, ang], -1)
    sign = jnp.concatenate([-jnp.ones((d // 2,), F32), jnp.ones((d // 2,), F32)])
    cos_t = jnp.cos(ang)
    sin_t = jnp.sin(ang) * sign[None, :]
    log_g = jnp.log1p(-(2.0 ** (-5.0 - jnp.arange(n_heads, dtype=F32))))
    idx = jnp.arange(t, dtype=F32)
    same_or_earlier_chunk = (jnp.floor(idx[None, :] / CHUNK) <= jnp.floor(idx[:, None] / CHUNK))
    dec = jnp.exp(log_g[:, None, None] * jnp.abs(idx[:, None] - idx[None, :])) * (d ** -0.5)
    dec = jnp.where(same_or_earlier_chunk[None], dec, 0.0)
    xi = jnp.broadcast_to(jnp.exp(log_g[:, None] * (idx + 1.0))[:, :, None], (n_heads, t, d))
    zeta = jnp.broadcast_to((jnp.exp(log_g[:, None] * (t - 1.0 - idx)) * (d ** -0.5))[:, :, None], (n_heads, t, d))
    gt = jnp.broadcast_to(jnp.exp(log_g * t)[:, None, None], (n_heads, 1, d))
    width = n_group * d
    group_spec = lambda part: pl.BlockSpec(
        (1, t, width), lambda bi, hg, i: (bi, i, (col0 + part * n_heads) // n_group + hg))
    per_head = lambda rows, cols: pl.BlockSpec((n_group, rows, cols), lambda bi, hg, i: (hg, 0, 0))
    return pl.pallas_call(
        functools.partial(_ret_kernel, n_group=n_group),
        grid=(b, n_heads // n_group, s // t),
        in_specs=[group_spec(0), group_spec(1), group_spec(2), group_spec(3),
                  pl.BlockSpec((t, d), lambda bi, hg, i: (i, 0)),
                  pl.BlockSpec((t, d), lambda bi, hg, i: (i, 0)),
                  per_head(t, t), per_head(t, d), per_head(t, d), per_head(1, d),
                  pl.BlockSpec((1, width), lambda bi, hg, i: (0, hg))],
        out_specs=pl.BlockSpec((1, t, width), lambda bi, hg, i: (bi, i, hg)),
        out_shape=jax.ShapeDtypeStruct((b, s, n_heads * d), BF16),
        scratch_shapes=[pltpu.VMEM((n_group, d, d), F32)],
        compiler_params=_params(("parallel", "parallel", "arbitrary")),
        name="retention",
    )(proj, proj, proj, proj, cos_t, sin_t, dec, xi, zeta, gt, norm_w.reshape(1, -1).astype(F32))


def _layer_norm(y, w, b):
    mu = jnp.mean(y, axis=-1, keepdims=True)
    cen = y - mu
    var = jnp.mean(cen * cen, axis=-1, keepdims=True)
    return cen * lax.rsqrt(var + LN_EPS) * w + b


def _outproj_ln_kernel(sb_ref, ret_ref, wo_ref, x_ref, lw_ref, lb_ref, o_ref, ob_ref, *, alpha, split):
    mix = jnp.dot(sb_ref[...], wo_ref[:split, :], preferred_element_type=F32)
    mix = mix + jnp.dot(ret_ref[...], wo_ref[split:, :], preferred_element_type=F32)
    y = _layer_norm(alpha * x_ref[...] + mix, lw_ref[...], lb_ref[...])
    o_ref[...] = y
    ob_ref[...] = y.astype(BF16)


def _outproj_ln(sb, ret, wo, x, lw, lb, alpha):
    m, d = x.shape
    split = sb.shape[1]
    bm = 256
    kern = functools.partial(_outproj_ln_kernel, alpha=alpha, split=split)
    row = lambda width: pl.BlockSpec((bm, width), lambda i: (i, 0))
    full = lambda shape: pl.BlockSpec(shape, lambda i: (0, 0))
    return pl.pallas_call(
        kern,
        grid=(m // bm,),
        in_specs=[row(split), row(ret.shape[1]), full(wo.shape), row(d), full((1, d)), full((1, d))],
        out_specs=[row(d), row(d)],
        out_shape=[jax.ShapeDtypeStruct((m, d), F32), jax.ShapeDtypeStruct((m, d), BF16)],
        compiler_params=_params(("parallel",)),
        name="outproj_ln",
    )(sb, ret, wo, x, lw.reshape(1, d), lb.reshape(1, d))


def _top_ranks(s, iota, k, exact_ties):
    n = s.shape[0]
    work = s
    rank = jnp.full(s.shape, float(k), F32)
    tops = []
    for r in range(k):
        m = jnp.max(work, axis=0, keepdims=True)
        sel = work == m
        if exact_ties:
            first = jnp.min(jnp.where(sel, iota, float(n)), axis=0, keepdims=True)
            sel = iota == first
        rank = jnp.where(sel, float(r), rank)
        work = jnp.where(sel, -jnp.inf, work)
        tops.append(m)
    return rank, tops


def _route_column(s1, s2, iota, ciota, widths, exact_ties):
    k = PEER_TOPK
    nk, lanes = s1.shape
    n_cand = sum(widths)
    n_pad = ciota.shape[0]
    rank1, top1 = _top_ranks(s1, iota, k, exact_ties)
    rank2, top2 = _top_ranks(s2, iota, k, exact_ties)
    top2_all = jnp.concatenate(top2, axis=0)
    cand = [top1[a] + top2_all[:widths[a]] for a in range(k)]
    if n_pad > n_cand:
        cand.append(jnp.full((n_pad - n_cand, lanes), -jnp.inf, F32))
    cand = jnp.concatenate(cand, axis=0)
    crank, _ = _top_ranks(cand, ciota, k, exact_ties)
    chosen = crank < float(k)
    best = top1[0] + top2[0]
    z = jnp.sum(jnp.where(chosen, jnp.exp(cand - best), 0.0), axis=0, keepdims=True)
    ones = jnp.where(chosen, 1.0, 0.0)
    cnt = jnp.zeros((nk, lanes), F32)
    off = 0
    for a in range(k):
        cnt_a = jnp.sum(ones[off:off + widths[a]], axis=0, keepdims=True)
        cnt = jnp.where(rank1 == float(a), cnt_a, cnt)
        off += widths[a]
    picked = (jnp.sum(jnp.where(rank1 < float(k), 1.0, 0.0), axis=0, keepdims=True)
              + jnp.sum(jnp.where(rank2 < float(k), 1.0, 0.0), axis=0, keepdims=True)
              + jnp.sum(ones, axis=0, keepdims=True))
    unique = picked == float(3 * k)
    return (cnt, jnp.exp(s1 - top1[0]), rank2, jnp.exp(s2 - top2[0]) / z), unique


def _router_kernel(q_ref, keys_ref, cnt_ref, e1_ref, rank2_ref, e2_ref, *, lanes, cols_per_iter):
    nk = PEER_KEYS
    k = PEER_TOPK
    iota = lax.broadcasted_iota(jnp.int32, (nk, lanes), 0).astype(F32)
    widths = [k // (a + 1) for a in range(k)]
    n_pad = -(-sum(widths) // F32_SUBLANES) * F32_SUBLANES
    ciota = lax.broadcasted_iota(jnp.int32, (n_pad, lanes), 0).astype(F32)
    out_refs = (cnt_ref, e1_ref, rank2_ref, e2_ref)

    def columns(group, _):
        pending = []
        for j in range(cols_per_iter):
            c = group * cols_per_iter + j
            sl = pl.ds(pl.multiple_of(c * lanes, lanes), lanes)
            q = q_ref[sl, :]
            half = q.shape[1] // 2
            s1 = lax.dot_general(keys_ref[0, 0], q[:, :half], NT_DIMS, preferred_element_type=F32)
            s2 = lax.dot_general(keys_ref[0, 1], q[:, half:], NT_DIMS, preferred_element_type=F32)
            outs, unique = _route_column(s1, s2, iota, ciota, widths, exact_ties=False)
            for ref, val in zip(out_refs, outs):
                ref[0, c] = val
            pending.append((c, s1, s2, jnp.min(jnp.where(unique, 1.0, 0.0))))
        for c, s1, s2, all_unique in pending:
            @pl.when(all_unique < 0.5)
            def _():
                exact, _ = _route_column(s1, s2, iota, ciota, widths, exact_ties=True)
                for ref, val in zip(out_refs, exact):
                    ref[0, c] = val
        return 0

    lax.fori_loop(0, q_ref.shape[0] // (lanes * cols_per_iter), columns, 0)


def _router(qp, sub_keys, tb):
    t = qp.shape[0]
    h, _, nk, half = sub_keys.shape
    out_f32 = jax.ShapeDtypeStruct((h, t // LANES, nk, LANES), F32)
    out_spec = pl.BlockSpec((1, tb // LANES, nk, LANES), lambda j, hh: (hh, j, 0, 0))
    return pl.pallas_call(
        functools.partial(_router_kernel, lanes=LANES, cols_per_iter=2),
        grid=(t // tb, h),
        in_specs=[pl.BlockSpec((tb, 2 * half), lambda j, hh: (j, hh)),
                  pl.BlockSpec((1, 2, nk, half), lambda j, hh: (hh, 0, 0, 0))],
        out_specs=[out_spec, out_spec, out_spec, out_spec],
        out_shape=[out_f32, out_f32, out_f32, out_f32],
        compiler_params=_params(("parallel", "parallel")),
        name="peer_router",
    )(qp, sub_keys)


def _row_to_packed_tile(ref, h, c, row, n_rows):
    rep = jnp.broadcast_to(ref[h, c, 0, row:row + 1, :], (BF16_SUBLANES, LANES))
    packed = rep.astype(BF16)
    return pltpu.repeat(packed, n_rows // BF16_SUBLANES, axis=0)


def _peer_kernel(x_ref, u_ref, v_ref, cnt_ref, e1_ref, rank2_in, e2_in, o_ref,
                 acc_ref, act_a, act_b, hid_a, hid_b, rank2_ref, e2_ref, *, n_sub, n_e, n_items):
    s = pl.program_id(0)
    vec_e = jnp.clip(s - 1, 0, n_items - 1) % n_e
    out_e = jnp.clip(s - 2, 0, n_items - 1) % n_e
    tb = x_ref.shape[0]
    nk = PEER_KEYS
    n_col = tb // LANES

    def e2_rows(h, c):
        return pl.ds(BF16_SUBLANES + (h * n_col + c) * nk, nk)

    @pl.when(s == 0)
    def _():
        for ref in (act_a, act_b, hid_a, hid_b):
            ref[...] = jnp.zeros_like(ref)

    @pl.when(vec_e == 0)
    def _():
        rank2_ref[...] = rank2_in[...].astype(BF16)
        for h in range(PEER_HEADS):
            for c in range(n_col):
                e2_ref[e2_rows(h, c), :] = e2_in[h, c].astype(BF16)

    @pl.when(out_e == 0)
    def _():
        acc_ref[...] = jnp.zeros_like(acc_ref)

    def gated_tile(act_r, hid_w, ii, c):
        gate = jnp.zeros((nk, LANES), BF16)
        for h in range(PEER_HEADS):
            cnt = _row_to_packed_tile(cnt_ref, h, c, ii, nk)
            e1 = _row_to_packed_tile(e1_ref, h, c, ii, nk)
            gate = gate + jnp.where(rank2_ref[h, c] < cnt, e2_ref[e2_rows(h, c), :] * e1, jnp.zeros((), BF16))
        a = act_r[ii * nk:(ii + 1) * nk, c * LANES:(c + 1) * LANES]
        gelu = 0.5 * a * (1.0 + lax.erf(a * (2.0 ** -0.5)))
        hid_w[c * LANES:(c + 1) * LANES, ii * nk:(ii + 1) * nk] = (gelu.astype(BF16) * gate).T

    def stages(act_w, act_r, hid_w, hid_r):
        half = tb // 2
        for p in range(2):
            act_w[:, p * half:(p + 1) * half] = lax.dot_general(
                u_ref[...], x_ref[p * half:(p + 1) * half, :], NT_DIMS, preferred_element_type=F32)
        tiles = [(ii, c) for ii in range(n_sub) for c in range(n_col)]
        n_chunks = v_ref.shape[1] // V7X_MXU_WIDTH
        per_chunk = -(-len(tiles) // n_chunks)
        for k in range(n_chunks):
            cols = slice(k * V7X_MXU_WIDTH, (k + 1) * V7X_MXU_WIDTH)
            acc_ref[:, cols] += jnp.dot(hid_r[...], v_ref[:, cols], preferred_element_type=F32)
            for ii, c in tiles[k * per_chunk:(k + 1) * per_chunk]:
                gated_tile(act_r, hid_w, ii, c)

    @pl.when(s % 2 == 0)
    def _():
        stages(act_a, act_b, hid_b, hid_a)

    @pl.when(s % 2 == 1)
    def _():
        stages(act_b, act_a, hid_a, hid_b)

    @pl.when((s >= 2) & (out_e == n_e - 1))
    def _():
        o_ref[...] = acc_ref[...]


def _peer_experts(xb, u, v, cnt, e1, rank2, e2, tb, eb):
    t, d = xb.shape
    n_exp = u.shape[0]
    h, n_col_all, nk, _ = cnt.shape
    assert eb == F32_SUBLANES * nk
    n_e = n_exp // eb
    n_items = (t // tb) * n_e
    assert n_items % 2 == 0
    mm_item = lambda s: jnp.minimum(s, n_items - 1)
    vec_item = lambda s: jnp.clip(s - 1, 0, n_items - 1)
    out_item = lambda s: jnp.clip(s - 2, 0, n_items - 1)
    route = pl.BlockSpec((h, tb // LANES, nk, LANES), lambda s: (0, vec_item(s) // n_e, 0, 0))
    first_key_rows = pl.BlockSpec((h, tb // LANES, 1, F32_SUBLANES, LANES),
                                  lambda s: (0, vec_item(s) // n_e, vec_item(s) % n_e, 0, 0))
    cnt, e1 = (a.reshape(h, n_col_all, nk // F32_SUBLANES, F32_SUBLANES, LANES) for a in (cnt, e1))
    return pl.pallas_call(
        functools.partial(_peer_kernel, n_sub=eb // nk, n_e=n_e, n_items=n_items),
        grid=(n_items + 2,),
        in_specs=[pl.BlockSpec((tb, d), lambda s: (mm_item(s) // n_e, 0)),
                  pl.BlockSpec((eb, d), lambda s: (mm_item(s) % n_e, 0)),
                  pl.BlockSpec((eb, d), lambda s: (out_item(s) % n_e, 0)),
                  first_key_rows, first_key_rows, route, route],
        out_specs=pl.BlockSpec((tb, d), lambda s: (out_item(s) // n_e, 0)),
        out_shape=jax.ShapeDtypeStruct((t, d), F32),
        scratch_shapes=[pltpu.VMEM((tb, d), F32),
                        pltpu.VMEM((eb, tb), F32), pltpu.VMEM((eb, tb), F32),
                        pltpu.VMEM((tb, eb), BF16), pltpu.VMEM((tb, eb), BF16),
                        pltpu.VMEM((h, tb // LANES, nk, LANES), BF16),
                        pltpu.VMEM((h * (tb // LANES) * nk + BF16_SUBLANES, LANES), BF16)],
        compiler_params=_params(("arbitrary",)),
        name="peer_experts",
    )(xb, u, v, cnt, e1, rank2, e2)


def _res_ln_kernel(x_ref, f_ref, lw_ref, lb_ref, o_ref, *, alpha):
    o_ref[...] = _layer_norm(alpha * x_ref[...] + f_ref[...], lw_ref[...], lb_ref[...])


def _res_ln(x, f, lw, lb, alpha):
    m, d = x.shape
    bm = 512
    row = pl.BlockSpec((bm, d), lambda i: (i, 0))
    full = pl.BlockSpec((1, d), lambda i: (0, 0))
    return pl.pallas_call(
        functools.partial(_res_ln_kernel, alpha=alpha),
        grid=(m // bm,),
        in_specs=[row, row, full, full],
        out_specs=row,
        out_shape=jax.ShapeDtypeStruct((m, d), F32),
        compiler_params=_params(("parallel",)),
        name="res_ln",
    )(x, f, lw.reshape(1, d), lb.reshape(1, d))


def _layer(x, w_in, sb_norm_w, ret_norm_w, w_out, ln1_w, ln1_b, w_pq, sub_keys, expert_u, expert_v,
           ln2_w, ln2_b, alpha):
    b, s, d = x.shape
    n_tok = b * s
    sb_heads = sb_norm_w.shape[0] // HEAD_DIM
    ret_heads = ret_norm_w.shape[0] // HEAD_DIM
    xt = x.reshape(n_tok, d)
    col_scale = jnp.where(jnp.arange(w_in.shape[1]) < sb_heads * HEAD_DIM, SB_Q_SCALE, 1.0).astype(F32)
    proj = _matmul(xt.astype(BF16), (w_in * col_scale).astype(BF16), 1024, 1024, BF16).reshape(b, s, -1)
    sb = _sb_attention(proj, sb_norm_w, sb_heads, 0)
    ret = _retention(proj, ret_norm_w, ret_heads, 3 * sb_heads)
    x1, x1b = _outproj_ln(sb.reshape(n_tok, -1), ret.reshape(n_tok, -1), w_out.astype(BF16), xt,
                          ln1_w, ln1_b, alpha)
    qp = _matmul(x1b, w_pq.astype(BF16), 1024, 1024, BF16)
    cnt, e1, rank2, e2 = _router(qp, sub_keys.astype(BF16), 512)
    ffn = _peer_experts(x1b, expert_u.astype(BF16), expert_v.astype(BF16), cnt, e1, rank2, e2, 512, 1024)
    return _res_ln(x1, ffn, ln2_w, ln2_b, alpha).reshape(b, s, d)


def kernel(x, w_in, sb_norm_w, ret_norm_w, w_out, ln1_w, ln1_b, w_pq, sub_keys, expert_u, expert_v, ln2_w, ln2_b):
    depth = w_in.shape[0]
    alpha = (2.0 * depth) ** 0.25
    for layer in range(depth):
        x = _layer(x, w_in[layer], sb_norm_w[layer], ret_norm_w[layer], w_out[layer], ln1_w[layer],
                   ln1_b[layer], w_pq[layer], sub_keys[layer], expert_u[layer], expert_v[layer],
                   ln2_w[layer], ln2_b[layer], alpha)
    return x
```

```python
import functools
import math

import jax
import jax.numpy as jnp
from jax import lax
from jax.experimental import pallas as pl
from jax.experimental.pallas import tpu as pltpu

F32 = jnp.float32
BF16 = jnp.bfloat16

LANES = 128
F32_SUBLANES = 8
BF16_SUBLANES = 16
HEAD_DIM = 128
CHUNK = 64
ROPE_THETA = 10000.0
PEER_HEADS = 8
PEER_KEYS = 128
PEER_TOPK = 16
LN_EPS = 1e-5
NORM_EPS = 1e-6
SB_Q_SCALE = HEAD_DIM ** -0.5 * math.log2(math.e)
F32_EXP2_UNDERFLOW = 160.0

V7X_VMEM_BYTES = 64 * 1024 * 1024
VMEM_LIMIT = V7X_VMEM_BYTES - 6 * 1024 * 1024

NT_DIMS = (((1,), (1,)), ((), ()))
TN_DIMS = (((0,), (0,)), ((), ()))


def _params(semantics):
    return pltpu.CompilerParams(dimension_semantics=semantics, vmem_limit_bytes=VMEM_LIMIT)


def _matmul_kernel(x_ref, w_ref, o_ref):
    o_ref[...] = jnp.dot(x_ref[...], w_ref[...], preferred_element_type=F32).astype(o_ref.dtype)


def _matmul(x, w, bm, bn, out_dtype):
    m, k = x.shape
    n = w.shape[1]
    return pl.pallas_call(
        _matmul_kernel,
        grid=(n // bn, m // bm),
        in_specs=[pl.BlockSpec((bm, k), lambda j, i: (i, 0)),
                  pl.BlockSpec((k, bn), lambda j, i: (0, j))],
        out_specs=pl.BlockSpec((bm, bn), lambda j, i: (i, j)),
        out_shape=jax.ShapeDtypeStruct((m, n), out_dtype),
        compiler_params=_params(("parallel", "parallel")),
        name="dense_proj",
    )(x, w)


def _sb_kernel(q_ref, k_ref, v_ref, tri_ref, w_ref, o_ref, acc_ref, run_ref, *, t, n_group):
    qi = pl.program_id(2)
    tri = tri_ref[...]
    below_diag = (lax.broadcasted_iota(jnp.int32, (t, t), 1) < lax.broadcasted_iota(jnp.int32, (t, t), 0))
    lanes = [slice(g * HEAD_DIM, (g + 1) * HEAD_DIM) for g in range(n_group)]

    def block(j, on_diagonal):
        start = pl.multiple_of(j * t, t)

        def scores(g):
            return lax.dot_general(q_ref[0, :, lanes[g]], k_ref[0, pl.ds(start, t), lanes[g]], NT_DIMS,
                                   preferred_element_type=F32)

        def suffix_sums(z):
            softplus = jnp.maximum(z, 0.0) + jnp.log2(1.0 + jnp.exp2(-jnp.abs(z)))
            log_beta = z - softplus
            if on_diagonal:
                softplus = jnp.where(below_diag, softplus, 0.0)
            hi = softplus.astype(BF16)
            lo = (softplus - hi.astype(F32)).astype(BF16)
            suffix = jnp.dot(jnp.concatenate([hi, lo], axis=1), tri, preferred_element_type=F32)
            return log_beta, suffix, jnp.sum(softplus, axis=1, keepdims=True)

        def weighted_values(g, log_beta, suffix, total):
            vs = v_ref[0, pl.ds(start, t), lanes[g]]
            if on_diagonal:
                a = jnp.where(below_diag, jnp.exp2(log_beta - suffix), 0.0)
                acc_ref[:, lanes[g]] = jnp.dot(a.astype(BF16), vs, preferred_element_type=F32)
                run_ref[g] = total
            else:
                run = run_ref[g]
                a = jnp.exp2(log_beta - suffix - run)
                acc_ref[:, lanes[g]] += jnp.dot(a.astype(BF16), vs, preferred_element_type=F32)
                run_ref[g] = run + total

        z, mid = {}, {}
        for step in range(n_group + 2):
            if step < n_group:
                z[step] = scores(step)
            if 0 <= step - 1 < n_group:
                mid[step - 1] = suffix_sums(z.pop(step - 1))
            if 0 <= step - 2 < n_group:
                weighted_values(step - 2, *mid.pop(step - 2))

    def smallest_run():
        m = run_ref[0]
        for g in range(1, n_group):
            m = jnp.minimum(m, run_ref[g])
        return jnp.min(m)

    block(qi, True)

    def more_to_do(state):
        jj, low = state
        return jnp.logical_and(jj <= qi, low < F32_EXP2_UNDERFLOW)

    def off_diagonal(state):
        jj, _ = state
        block(qi - jj, False)
        return jj + 1, smallest_run()

    lax.while_loop(more_to_do, off_diagonal, (jnp.int32(1), smallest_run()))
    for g in range(n_group):
        acc = acc_ref[:, lanes[g]]
        ms = jnp.mean(acc * acc, axis=-1, keepdims=True)
        o_ref[0, :, lanes[g]] = (acc * lax.rsqrt(ms + NORM_EPS) * w_ref[:, lanes[g]]).astype(o_ref.dtype)


def _sb_attention(proj, norm_w, n_heads, col0, n_group=8, t=256):
    b, s, _ = proj.shape
    width = n_group * HEAD_DIM
    strict_lower = (jnp.arange(t)[:, None] > jnp.arange(t)[None, :]).astype(BF16)
    tri = jnp.concatenate([strict_lower, strict_lower], axis=0)
    kern = functools.partial(_sb_kernel, t=t, n_group=n_group)
    group_col = lambda part: (col0 + part * n_heads) // n_group
    return pl.pallas_call(
        kern,
        grid=(b, n_heads // n_group, s // t),
        in_specs=[pl.BlockSpec((1, t, width), lambda bi, hg, i: (bi, i, group_col(0) + hg)),
                  pl.BlockSpec((1, s, width), lambda bi, hg, i: (bi, 0, group_col(1) + hg)),
                  pl.BlockSpec((1, s, width), lambda bi, hg, i: (bi, 0, group_col(2) + hg)),
                  pl.BlockSpec((2 * t, t), lambda bi, hg, i: (0, 0)),
                  pl.BlockSpec((1, width), lambda bi, hg, i: (0, hg))],
        out_specs=pl.BlockSpec((1, t, width), lambda bi, hg, i: (bi, i, hg)),
        out_shape=jax.ShapeDtypeStruct((b, s, n_heads * HEAD_DIM), BF16),
        scratch_shapes=[pltpu.VMEM((t, width), F32), pltpu.VMEM((n_group, t, 1), F32)],
        compiler_params=_params(("parallel", "parallel", "arbitrary")),
        name="sb_attention",
    )(proj, proj, proj, tri, norm_w.reshape(1, -1).astype(F32))


def _ret_kernel(q_ref, k_ref, v_ref, g_ref, cos_ref, sin_ref, dec_ref, xi_ref, zeta_ref, gt_ref, w_ref,
                o_ref, state_ref, *, n_group):
    @pl.when(pl.program_id(2) == 0)
    def _():
        state_ref[...] = jnp.zeros_like(state_ref)

    cos = cos_ref[...]
    sin = sin_ref[...]
    half = HEAD_DIM // 2
    lanes = [slice(g * HEAD_DIM, (g + 1) * HEAD_DIM) for g in range(n_group)]

    def rotate_and_score(g):
        q = q_ref[0, :, lanes[g]].astype(F32)
        k = k_ref[0, :, lanes[g]].astype(F32)
        qr = q * cos + pltpu.roll(q, half, 1) * sin
        kr = k * cos + pltpu.roll(k, half, 1) * sin
        qb = qr.astype(BF16)
        scores = lax.dot_general(qb, kr.astype(BF16), NT_DIMS, preferred_element_type=F32)
        carried = jnp.dot(qb, state_ref[g].astype(BF16), preferred_element_type=F32)
        return kr, scores, carried

    def mix_values(g, kr, scores, carried):
        v = v_ref[0, :, lanes[g]]
        ret = jnp.dot((scores * dec_ref[g]).astype(BF16), v, preferred_element_type=F32) + carried * xi_ref[g]
        kz = (kr * zeta_ref[g]).astype(BF16)
        state_ref[g] = state_ref[g] * gt_ref[g] + lax.dot_general(kz, v, TN_DIMS, preferred_element_type=F32)
        return ret

    def normalise(g, ret):
        mu = jnp.mean(ret, axis=-1, keepdims=True)
        cen = ret - mu
        var = jnp.mean(cen * cen, axis=-1, keepdims=True)
        gate = g_ref[0, :, lanes[g]].astype(F32)
        silu = gate / (1.0 + jnp.exp(-gate))
        o_ref[0, :, lanes[g]] = (cen * lax.rsqrt(var + NORM_EPS) * w_ref[:, lanes[g]] * silu).astype(o_ref.dtype)

    first, second = {}, {}
    for step in range(n_group + 2):
        if step < n_group:
            first[step] = rotate_and_score(step)
        if 0 <= step - 1 < n_group:
            second[step - 1] = mix_values(step - 1, *first.pop(step - 1))
        if 0 <= step - 2 < n_group:
            normalise(step - 2, second.pop(step - 2))


def _retention(proj, norm_w, n_heads, col0, n_group=4):
    b, s, _ = proj.shape
    t = 256
    d = HEAD_DIM
    inv_freq = ROPE_THETA ** (-jnp.arange(0, d, 2, dtype=F32) / d)
    ang = jnp.arange(s, dtype=F32)[:, None] * inv_freq[None, :]
    ang = jnp.concatenate([ang, ang], -1)
    sign = jnp.concatenate([-jnp.ones((d // 2,), F32), jnp.ones((d // 2,), F32)])
    cos_t = jnp.cos(ang)
    sin_t = jnp.sin(ang) * sign[None, :]
    log_g = jnp.log1p(-(2.0 ** (-5.0 - jnp.arange(n_heads, dtype=F32))))
    idx = jnp.arange(t, dtype=F32)
    same_or_earlier_chunk = (jnp.floor(idx[None, :] / CHUNK) <= jnp.floor(idx[:, None] / CHUNK))
    dec = jnp.exp(log_g[:, None, None] * jnp.abs(idx[:, None] - idx[None, :])) * (d ** -0.5)
    dec = jnp.where(same_or_earlier_chunk[None], dec, 0.0)
    xi = jnp.broadcast_to(jnp.exp(log_g[:, None] * (idx + 1.0))[:, :, None], (n_heads, t, d))
    zeta = jnp.broadcast_to((jnp.exp(log_g[:, None] * (t - 1.0 - idx)) * (d ** -0.5))[:, :, None], (n_heads, t, d))
    gt = jnp.broadcast_to(jnp.exp(log_g * t)[:, None, None], (n_heads, 1, d))
    width = n_group * d
    group_spec = lambda part: pl.BlockSpec(
        (1, t, width), lambda bi, hg, i: (bi, i, (col0 + part * n_heads) // n_group + hg))
    per_head = lambda rows, cols: pl.BlockSpec((n_group, rows, cols), lambda bi, hg, i: (hg, 0, 0))
    return pl.pallas_call(
        functools.partial(_ret_kernel, n_group=n_group),
        grid=(b, n_heads // n_group, s // t),
        in_specs=[group_spec(0), group_spec(1), group_spec(2), group_spec(3),
                  pl.BlockSpec((t, d), lambda bi, hg, i: (i, 0)),
                  pl.BlockSpec((t, d), lambda bi, hg, i: (i, 0)),
                  per_head(t, t), per_head(t, d), per_head(t, d), per_head(1, d),
                  pl.BlockSpec((1, width), lambda bi, hg, i: (0, hg))],
        out_specs=pl.BlockSpec((1, t, width), lambda bi, hg, i: (bi, i, hg)),
        out_shape=jax.ShapeDtypeStruct((b, s, n_heads * d), BF16),
        scratch_shapes=[pltpu.VMEM((n_group, d, d), F32)],
        compiler_params=_params(("parallel", "parallel", "arbitrary")),
        name="retention",
    )(proj, proj, proj, proj, cos_t, sin_t, dec, xi, zeta, gt, norm_w.reshape(1, -1).astype(F32))


def _layer_norm(y, w, b):
    mu = jnp.mean(y, axis=-1, keepdims=True)
    cen = y - mu
    var = jnp.mean(cen * cen, axis=-1, keepdims=True)
    return cen * lax.rsqrt(var + LN_EPS) * w + b


def _outproj_ln_kernel(sb_ref, ret_ref, wo_ref, x_ref, lw_ref, lb_ref, o_ref, ob_ref, *, alpha, split):
    mix = jnp.dot(sb_ref[...], wo_ref[:split, :], preferred_element_type=F32)
    mix = mix + jnp.dot(ret_ref[...], wo_ref[split:, :], preferred_element_type=F32)
    y = _layer_norm(alpha * x_ref[...] + mix, lw_ref[...], lb_ref[...])
    o_ref[...] = y
    ob_ref[...] = y.astype(BF16)


def _outproj_ln(sb, ret, wo, x, lw, lb, alpha):
    m, d = x.shape
    split = sb.shape[1]
    bm = 256
    kern = functools.partial(_outproj_ln_kernel, alpha=alpha, split=split)
    row = lambda width: pl.BlockSpec((bm, width), lambda i: (i, 0))
    full = lambda shape: pl.BlockSpec(shape, lambda i: (0, 0))
    return pl.pallas_call(
        kern,
        grid=(m // bm,),
        in_specs=[row(split), row(ret.shape[1]), full(wo.shape), row(d), full((1, d)), full((1, d))],
        out_specs=[row(d), row(d)],
        out_shape=[jax.ShapeDtypeStruct((m, d), F32), jax.ShapeDtypeStruct((m, d), BF16)],
        compiler_params=_params(("parallel",)),
        name="outproj_ln",
    )(sb, ret, wo, x, lw.reshape(1, d), lb.reshape(1, d))


def _top_ranks(s, iota, k, exact_ties):
    n = s.shape[0]
    work = s
    rank = jnp.full(s.shape, float(k), F32)
    tops = []
    for r in range(k):
        m = jnp.max(work, axis=0, keepdims=True)
        sel = work == m
        if exact_ties:
            first = jnp.min(jnp.where(sel, iota, float(n)), axis=0, keepdims=True)
            sel = iota == first
        rank = jnp.where(sel, float(r), rank)
        work = jnp.where(sel, -jnp.inf, work)
        tops.append(m)
    return rank, tops


def _route_column(s1, s2, iota, ciota, widths, exact_ties):
    k = PEER_TOPK
    nk, lanes = s1.shape
    n_cand = sum(widths)
    n_pad = ciota.shape[0]
    rank1, top1 = _top_ranks(s1, iota, k, exact_ties)
    rank2, top2 = _top_ranks(s2, iota, k, exact_ties)
    top2_all = jnp.concatenate(top2, axis=0)
    cand = [top1[a] + top2_all[:widths[a]] for a in range(k)]
    if n_pad > n_cand:
        cand.append(jnp.full((n_pad - n_cand, lanes), -jnp.inf, F32))
    cand = jnp.concatenate(cand, axis=0)
    crank, _ = _top_ranks(cand, ciota, k, exact_ties)
    chosen = crank < float(k)
    best = top1[0] + top2[0]
    z = jnp.sum(jnp.where(chosen, jnp.exp(cand - best), 0.0), axis=0, keepdims=True)
    ones = jnp.where(chosen, 1.0, 0.0)
    cnt = jnp.zeros((nk, lanes), F32)
    off = 0
    for a in range(k):
        cnt_a = jnp.sum(ones[off:off + widths[a]], axis=0, keepdims=True)
        cnt = jnp.where(rank1 == float(a), cnt_a, cnt)
        off += widths[a]
    picked = (jnp.sum(jnp.where(rank1 < float(k), 1.0, 0.0), axis=0, keepdims=True)
              + jnp.sum(jnp.where(rank2 < float(k), 1.0, 0.0), axis=0, keepdims=True)
              + jnp.sum(ones, axis=0, keepdims=True))
    unique = picked == float(3 * k)
    return (cnt, jnp.exp(s1 - top1[0]), rank2, jnp.exp(s2 - top2[0]) / z), unique


def _router_kernel(q_ref, keys_ref, cnt_ref, e1_ref, rank2_ref, e2_ref, *, lanes, cols_per_iter):
    nk = PEER_KEYS
    k = PEER_TOPK
    iota = lax.broadcasted_iota(jnp.int32, (nk, lanes), 0).astype(F32)
    widths = [k // (a + 1) for a in range(k)]
    n_pad = -(-sum(widths) // F32_SUBLANES) * F32_SUBLANES
    ciota = lax.broadcasted_iota(jnp.int32, (n_pad, lanes), 0).astype(F32)
    out_refs = (cnt_ref, e1_ref, rank2_ref, e2_ref)

    def columns(group, _):
        pending = []
        for j in range(cols_per_iter):
            c = group * cols_per_iter + j
            sl = pl.ds(pl.multiple_of(c * lanes, lanes), lanes)
            q = q_ref[sl, :]
            half = q.shape[1] // 2
            s1 = lax.dot_general(keys_ref[0, 0], q[:, :half], NT_DIMS, preferred_element_type=F32)
            s2 = lax.dot_general(keys_ref[0, 1], q[:, half:], NT_DIMS, preferred_element_type=F32)
            outs, unique = _route_column(s1, s2, iota, ciota, widths, exact_ties=False)
            for ref, val in zip(out_refs, outs):
                ref[0, c] = val
            pending.append((c, s1, s2, jnp.min(jnp.where(unique, 1.0, 0.0))))
        for c, s1, s2, all_unique in pending:
            @pl.when(all_unique < 0.5)
            def _():
                exact, _ = _route_column(s1, s2, iota, ciota, widths, exact_ties=True)
                for ref, val in zip(out_refs, exact):
                    ref[0, c] = val
        return 0

    lax.fori_loop(0, q_ref.shape[0] // (lanes * cols_per_iter), columns, 0)


def _router(qp, sub_keys, tb):
    t = qp.shape[0]
    h, _, nk, half = sub_keys.shape
    out_f32 = jax.ShapeDtypeStruct((h, t // LANES, nk, LANES), F32)
    out_spec = pl.BlockSpec((1, tb // LANES, nk, LANES), lambda j, hh: (hh, j, 0, 0))
    return pl.pallas_call(
        functools.partial(_router_kernel, lanes=LANES, cols_per_iter=2),
        grid=(t // tb, h),
        in_specs=[pl.BlockSpec((tb, 2 * half), lambda j, hh: (j, hh)),
                  pl.BlockSpec((1, 2, nk, half), lambda j, hh: (hh, 0, 0, 0))],
        out_specs=[out_spec, out_spec, out_spec, out_spec],
        out_shape=[out_f32, out_f32, out_f32, out_f32],
        compiler_params=_params(("parallel", "parallel")),
        name="peer_router",
    )(qp, sub_keys)


def _row_to_packed_tile(ref, h, c, row, n_rows):
    rep = jnp.broadcast_to(ref[h, c, 0, row:row + 1, :], (BF16_SUBLANES, LANES))
    packed = rep.astype(BF16)
    return jnp.concatenate([packed] * (n_rows // BF16_SUBLANES), axis=0)


def _peer_kernel(xb_ref, u_ref, v_ref, cnt_ref, e1_ref, rank2_in, e2_in, x_ref, lw_ref, lb_ref, o_ref,
                 act_ref, hid_ref, rank2_ref, e2_ref, *, n_sub, alpha):
    e = pl.program_id(1)
    tb = xb_ref.shape[0]
    nk = PEER_KEYS
    n_col = tb // LANES

    @pl.when(e == 0)
    def _():
        rank2_ref[...] = rank2_in[...].astype(BF16)
        e2_ref[...] = e2_in[...].astype(BF16)
        o_ref[...] = jnp.zeros_like(o_ref)

    act_ref[...] = lax.dot_general(u_ref[...], xb_ref[...], NT_DIMS, preferred_element_type=F32)
    for ii in range(n_sub):
        for c in range(n_col):
            gate = jnp.zeros((nk, LANES), BF16)
            for h in range(PEER_HEADS):
                cnt = _row_to_packed_tile(cnt_ref, h, c, ii, nk)
                e1 = _row_to_packed_tile(e1_ref, h, c, ii, nk)
                gate = gate + jnp.where(rank2_ref[h, c] < cnt, e2_ref[h, c] * e1, jnp.zeros((), BF16))
            a = act_ref[ii * nk:(ii + 1) * nk, c * LANES:(c + 1) * LANES]
            gelu = 0.5 * a * (1.0 + lax.erf(a * (2.0 ** -0.5)))
            hid_ref[c * LANES:(c + 1) * LANES, ii * nk:(ii + 1) * nk] = (gelu.astype(BF16) * gate).T
    o_ref[...] += jnp.dot(hid_ref[...], v_ref[...], preferred_element_type=F32)

    @pl.when(e == pl.num_programs(1) - 1)
    def _():
        o_ref[...] = _layer_norm(alpha * x_ref[...] + o_ref[...], lw_ref[...], lb_ref[...])


def _peer_ffn_ln(xb, x, u, v, cnt, e1, rank2, e2, lw, lb, alpha, tb, eb):
    t, d = xb.shape
    n_exp = u.shape[0]
    h, n_col_all, nk, _ = cnt.shape
    assert eb == F32_SUBLANES * nk
    tokens = pl.BlockSpec((tb, d), lambda j, e: (j, 0))
    table = pl.BlockSpec((eb, d), lambda j, e: (e, 0))
    route = pl.BlockSpec((h, tb // LANES, nk, LANES), lambda j, e: (0, j, 0, 0))
    first_key_rows = pl.BlockSpec((h, tb // LANES, 1, F32_SUBLANES, LANES), lambda j, e: (0, j, e, 0, 0))
    vector = pl.BlockSpec((1, d), lambda j, e: (0, 0))
    cnt, e1 = (a.reshape(h, n_col_all, nk // F32_SUBLANES, F32_SUBLANES, LANES) for a in (cnt, e1))
    packed = pltpu.VMEM((h, tb // LANES, nk, LANES), BF16)
    return pl.pallas_call(
        functools.partial(_peer_kernel, n_sub=eb // nk, alpha=alpha),
        grid=(t // tb, n_exp // eb),
        in_specs=[tokens, table, table, first_key_rows, first_key_rows, route, route, tokens, vector, vector],
        out_specs=tokens,
        out_shape=jax.ShapeDtypeStruct((t, d), F32),
        scratch_shapes=[pltpu.VMEM((eb, tb), F32), pltpu.VMEM((tb, eb), BF16), packed, packed],
        compiler_params=_params(("parallel", "arbitrary")),
        name="peer_ffn_ln",
    )(xb, u, v, cnt, e1, rank2, e2, x, lw.reshape(1, d), lb.reshape(1, d))


def _layer(x, w_in, sb_norm_w, ret_norm_w, w_out, ln1_w, ln1_b, w_pq, sub_keys, expert_u, expert_v,
           ln2_w, ln2_b, alpha):
    b, s, d = x.shape
    n_tok = b * s
    sb_heads = sb_norm_w.shape[0] // HEAD_DIM
    ret_heads = ret_norm_w.shape[0] // HEAD_DIM
    xt = x.reshape(n_tok, d)
    col_scale = jnp.where(jnp.arange(w_in.shape[1]) < sb_heads * HEAD_DIM, SB_Q_SCALE, 1.0).astype(F32)
    proj = _matmul(xt.astype(BF16), (w_in * col_scale).astype(BF16), 1024, 1024, BF16).reshape(b, s, -1)
    sb = _sb_attention(proj, sb_norm_w, sb_heads, 0)
    ret = _retention(proj, ret_norm_w, ret_heads, 3 * sb_heads)
    x1, x1b = _outproj_ln(sb.reshape(n_tok, -1), ret.reshape(n_tok, -1), w_out.astype(BF16), xt,
                          ln1_w, ln1_b, alpha)
    qp = _matmul(x1b, w_pq.astype(BF16), 1024, 1024, BF16)
    cnt, e1, rank2, e2 = _router(qp, sub_keys.astype(BF16), 512)
    out = _peer_ffn_ln(x1b, x1, expert_u.astype(BF16), expert_v.astype(BF16), cnt, e1, rank2, e2, ln2_w, ln2_b,
                       alpha, 512, 1024)
    return out.reshape(b, s, d)


def kernel(x, w_in, sb_norm_w, ret_norm_w, w_out, ln1_w, ln1_b, w_pq, sub_keys, expert_u, expert_v, ln2_w, ln2_b):
    depth = w_in.shape[0]
    alpha = (2.0 * depth) ** 0.25
    for layer in range(depth):
        x = _layer(x, w_in[layer], sb_norm_w[layer], ret_norm_w[layer], w_out[layer], ln1_w[layer],
                   ln1_b[layer], w_pq[layer], sub_keys[layer], expert_u[layer], expert_v[layer],
                   ln2_w[layer], ln2_b[layer], alpha)
    return x
```

```python
import functools
import math

import jax
import jax.numpy as jnp
from jax import lax
from jax.experimental import pallas as pl
from jax.experimental.pallas import tpu as pltpu

F32 = jnp.float32
BF16 = jnp.bfloat16

LANES = 128
F32_SUBLANES = 8
BF16_SUBLANES = 16
HEAD_DIM = 128
CHUNK = 64
ROPE_THETA = 10000.0
PEER_HEADS = 8
PEER_KEYS = 128
PEER_TOPK = 16
LN_EPS = 1e-5
NORM_EPS = 1e-6
SB_Q_SCALE = HEAD_DIM ** -0.5 * math.log2(math.e)
F32_EXP2_UNDERFLOW = 160.0

V7X_VMEM_BYTES = 64 * 1024 * 1024
VMEM_LIMIT = V7X_VMEM_BYTES - 6 * 1024 * 1024

NT_DIMS = (((1,), (1,)), ((), ()))
TN_DIMS = (((0,), (0,)), ((), ()))


def _params(semantics):
    return pltpu.CompilerParams(dimension_semantics=semantics, vmem_limit_bytes=VMEM_LIMIT)


def _matmul_kernel(x_ref, w_ref, *rest, scaled):
    scale_ref = rest[0] if scaled else None
    o_ref, wb_ref = rest[-2:]

    @pl.when(pl.program_id(1) == 0)
    def _():
        w = w_ref[...] * scale_ref[...] if scaled else w_ref[...]
        wb_ref[...] = w.astype(BF16)

    o_ref[...] = jnp.dot(x_ref[...], wb_ref[...], preferred_element_type=F32).astype(o_ref.dtype)


def _matmul(x, w, bm, bn, out_dtype, col_scale=None):
    m, k = x.shape
    n = w.shape[1]
    scaled = col_scale is not None
    in_specs = [pl.BlockSpec((bm, k), lambda j, i: (i, 0)), pl.BlockSpec((k, bn), lambda j, i: (0, j))]
    args = [x, w]
    if scaled:
        in_specs.append(pl.BlockSpec((1, bn), lambda j, i: (0, j)))
        args.append(col_scale.reshape(1, n))
    return pl.pallas_call(
        functools.partial(_matmul_kernel, scaled=scaled),
        grid=(n // bn, m // bm),
        in_specs=in_specs,
        out_specs=pl.BlockSpec((bm, bn), lambda j, i: (i, j)),
        out_shape=jax.ShapeDtypeStruct((m, n), out_dtype),
        scratch_shapes=[pltpu.VMEM((k, bn), BF16)],
        compiler_params=_params(("parallel", "arbitrary")),
        name="dense_proj",
    )(*args)


def _sb_kernel(q_ref, k_ref, v_ref, tri_ref, w_ref, o_ref, acc_ref, run_ref, *, t, n_group):
    qi = pl.program_id(2)
    tri = tri_ref[...]
    below_diag = (lax.broadcasted_iota(jnp.int32, (t, t), 1) < lax.broadcasted_iota(jnp.int32, (t, t), 0))
    lanes = [slice(g * HEAD_DIM, (g + 1) * HEAD_DIM) for g in range(n_group)]

    def block(j, on_diagonal):
        start = pl.multiple_of(j * t, t)

        def scores(g):
            return lax.dot_general(q_ref[0, :, lanes[g]], k_ref[0, pl.ds(start, t), lanes[g]], NT_DIMS,
                                   preferred_element_type=F32)

        def suffix_sums(z):
            softplus = jnp.maximum(z, 0.0) + jnp.log2(1.0 + jnp.exp2(-jnp.abs(z)))
            log_beta = z - softplus
            if on_diagonal:
                softplus = jnp.where(below_diag, softplus, 0.0)
            hi = softplus.astype(BF16)
            lo = (softplus - hi.astype(F32)).astype(BF16)
            suffix = jnp.dot(jnp.concatenate([hi, lo], axis=1), tri, preferred_element_type=F32)
            return log_beta, suffix, jnp.sum(softplus, axis=1, keepdims=True)

        def weighted_values(g, log_beta, suffix, total):
            vs = v_ref[0, pl.ds(start, t), lanes[g]]
            if on_diagonal:
                a = jnp.where(below_diag, jnp.exp2(log_beta - suffix), 0.0)
                acc_ref[:, lanes[g]] = jnp.dot(a.astype(BF16), vs, preferred_element_type=F32)
                run_ref[g] = total
            else:
                run = run_ref[g]
                a = jnp.exp2(log_beta - suffix - run)
                acc_ref[:, lanes[g]] += jnp.dot(a.astype(BF16), vs, preferred_element_type=F32)
                run_ref[g] = run + total

        z, mid = {}, {}
        for step in range(n_group + 2):
            if step < n_group:
                z[step] = scores(step)
            if 0 <= step - 1 < n_group:
                mid[step - 1] = suffix_sums(z.pop(step - 1))
            if 0 <= step - 2 < n_group:
                weighted_values(step - 2, *mid.pop(step - 2))

    def smallest_run():
        m = run_ref[0]
        for g in range(1, n_group):
            m = jnp.minimum(m, run_ref[g])
        return jnp.min(m)

    block(qi, True)

    def more_to_do(state):
        jj, low = state
        return jnp.logical_and(jj <= qi, low < F32_EXP2_UNDERFLOW)

    def off_diagonal(state):
        jj, _ = state
        block(qi - jj, False)
        return jj + 1, smallest_run()

    lax.while_loop(more_to_do, off_diagonal, (jnp.int32(1), smallest_run()))
    for g in range(n_group):
        acc = acc_ref[:, lanes[g]]
        ms = jnp.mean(acc * acc, axis=-1, keepdims=True)
        o_ref[0, :, lanes[g]] = (acc * lax.rsqrt(ms + NORM_EPS) * w_ref[:, lanes[g]]).astype(o_ref.dtype)


def _sb_attention(proj, norm_w, n_heads, col0, n_group=8, t=256):
    b, s, _ = proj.shape
    width = n_group * HEAD_DIM
    strict_lower = (jnp.arange(t)[:, None] > jnp.arange(t)[None, :]).astype(BF16)
    tri = jnp.concatenate([strict_lower, strict_lower], axis=0)
    kern = functools.partial(_sb_kernel, t=t, n_group=n_group)
    group_col = lambda part: (col0 + part * n_heads) // n_group
    return pl.pallas_call(
        kern,
        grid=(b, n_heads // n_group, s // t),
        in_specs=[pl.BlockSpec((1, t, width), lambda bi, hg, i: (bi, i, group_col(0) + hg)),
                  pl.BlockSpec((1, s, width), lambda bi, hg, i: (bi, 0, group_col(1) + hg)),
                  pl.BlockSpec((1, s, width), lambda bi, hg, i: (bi, 0, group_col(2) + hg)),
                  pl.BlockSpec((2 * t, t), lambda bi, hg, i: (0, 0)),
                  pl.BlockSpec((1, width), lambda bi, hg, i: (0, hg))],
        out_specs=pl.BlockSpec((1, t, width), lambda bi, hg, i: (bi, i, hg)),
        out_shape=jax.ShapeDtypeStruct((b, s, n_heads * HEAD_DIM), BF16),
        scratch_shapes=[pltpu.VMEM((t, width), F32), pltpu.VMEM((n_group, t, 1), F32)],
        compiler_params=_params(("parallel", "parallel", "arbitrary")),
        name="sb_attention",
    )(proj, proj, proj, tri, norm_w.reshape(1, -1).astype(F32))


def _ret_kernel(q_ref, k_ref, v_ref, g_ref, cos_ref, sin_ref, dec_ref, xi_ref, zeta_ref, gt_ref, w_ref,
                o_ref, state_ref, *, n_group):
    @pl.when(pl.program_id(2) == 0)
    def _():
        state_ref[...] = jnp.zeros_like(state_ref)

    cos = cos_ref[...]
    sin = sin_ref[...]
    half = HEAD_DIM // 2
    lanes = [slice(g * HEAD_DIM, (g + 1) * HEAD_DIM) for g in range(n_group)]

    def rotate_and_score(g):
        q = q_ref[0, :, lanes[g]].astype(F32)
        k = k_ref[0, :, lanes[g]].astype(F32)
        qr = q * cos + pltpu.roll(q, half, 1) * sin
        kr = k * cos + pltpu.roll(k, half, 1) * sin
        qb = qr.astype(BF16)
        scores = lax.dot_general(qb, kr.astype(BF16), NT_DIMS, preferred_element_type=F32)
        carried = jnp.dot(qb, state_ref[g].astype(BF16), preferred_element_type=F32)
        return kr, scores, carried

    def mix_values(g, kr, scores, carried):
        v = v_ref[0, :, lanes[g]]
        ret = jnp.dot((scores * dec_ref[g]).astype(BF16), v, preferred_element_type=F32) + carried * xi_ref[g]
        kz = (kr * zeta_ref[g]).astype(BF16)
        state_ref[g] = state_ref[g] * gt_ref[g] + lax.dot_general(kz, v, TN_DIMS, preferred_element_type=F32)
        return ret

    def normalise(g, ret):
        mu = jnp.mean(ret, axis=-1, keepdims=True)
        cen = ret - mu
        var = jnp.mean(cen * cen, axis=-1, keepdims=True)
        gate = g_ref[0, :, lanes[g]].astype(F32)
        silu = gate / (1.0 + jnp.exp(-gate))
        o_ref[0, :, lanes[g]] = (cen * lax.rsqrt(var + NORM_EPS) * w_ref[:, lanes[g]] * silu).astype(o_ref.dtype)

    first, second = {}, {}
    for step in range(n_group + 2):
        if step < n_group:
            first[step] = rotate_and_score(step)
        if 0 <= step - 1 < n_group:
            second[step - 1] = mix_values(step - 1, *first.pop(step - 1))
        if 0 <= step - 2 < n_group:
            normalise(step - 2, second.pop(step - 2))


def _retention(proj, norm_w, n_heads, col0, n_group=4):
    b, s, _ = proj.shape
    t = 256
    d = HEAD_DIM
    inv_freq = ROPE_THETA ** (-jnp.arange(0, d, 2, dtype=F32) / d)
    ang = jnp.arange(s, dtype=F32)[:, None] * inv_freq[None, :]
    ang = jnp.concatenate([ang, ang], -1)
    sign = jnp.concatenate([-jnp.ones((d // 2,), F32), jnp.ones((d // 2,), F32)])
    cos_t = jnp.cos(ang)
    sin_t = jnp.sin(ang) * sign[None, :]
    log_g = jnp.log1p(-(2.0 ** (-5.0 - jnp.arange(n_heads, dtype=F32))))
    idx = jnp.arange(t, dtype=F32)
    same_or_earlier_chunk = (jnp.floor(idx[None, :] / CHUNK) <= jnp.floor(idx[:, None] / CHUNK))
    dec = jnp.exp(log_g[:, None, None] * jnp.abs(idx[:, None] - idx[None, :])) * (d ** -0.5)
    dec = jnp.where(same_or_earlier_chunk[None], dec, 0.0)
    xi = jnp.broadcast_to(jnp.exp(log_g[:, None] * (idx + 1.0))[:, :, None], (n_heads, t, d))
    zeta = jnp.broadcast_to((jnp.exp(log_g[:, None] * (t - 1.0 - idx)) * (d ** -0.5))[:, :, None], (n_heads, t, d))
    gt = jnp.broadcast_to(jnp.exp(log_g * t)[:, None, None], (n_heads, 1, d))
    width = n_group * d
    group_spec = lambda part: pl.BlockSpec(
        (1, t, width), lambda bi, hg, i: (bi, i, (col0 + part * n_heads) // n_group + hg))
    per_head = lambda rows, cols: pl.BlockSpec((n_group, rows, cols), lambda bi, hg, i: (hg, 0, 0))
    return pl.pallas_call(
        functools.partial(_ret_kernel, n_group=n_group),
        grid=(b, n_heads // n_group, s // t),
        in_specs=[group_spec(0), group_spec(1), group_spec(2), group_spec(3),
                  pl.BlockSpec((t, d), lambda bi, hg, i: (i, 0)),
                  pl.BlockSpec((t, d), lambda bi, hg, i: (i, 0)),
                  per_head(t, t), per_head(t, d), per_head(t, d), per_head(1, d),
                  pl.BlockSpec((1, width), lambda bi, hg, i: (0, hg))],
        out_specs=pl.BlockSpec((1, t, width), lambda bi, hg, i: (bi, i, hg)),
        out_shape=jax.ShapeDtypeStruct((b, s, n_heads * d), BF16),
        scratch_shapes=[pltpu.VMEM((n_group, d, d), F32)],
        compiler_params=_params(("parallel", "parallel", "arbitrary")),
        name="retention",
    )(proj, proj, proj, proj, cos_t, sin_t, dec, xi, zeta, gt, norm_w.reshape(1, -1).astype(F32))


def _layer_norm(y, w, b):
    mu = jnp.mean(y, axis=-1, keepdims=True)
    cen = y - mu
    var = jnp.mean(cen * cen, axis=-1, keepdims=True)
    return cen * lax.rsqrt(var + LN_EPS) * w + b


def _outproj_ln_kernel(sb_ref, ret_ref, wo_ref, x_ref, lw_ref, lb_ref, o_ref, ob_ref, *, alpha, split):
    mix = jnp.dot(sb_ref[...], wo_ref[:split, :], preferred_element_type=F32)
    mix = mix + jnp.dot(ret_ref[...], wo_ref[split:, :], preferred_element_type=F32)
    y = _layer_norm(alpha * x_ref[...] + mix, lw_ref[...], lb_ref[...])
    o_ref[...] = y
    ob_ref[...] = y.astype(BF16)


def _outproj_ln(sb, ret, wo, x, lw, lb, alpha):
    m, d = x.shape
    split = sb.shape[1]
    bm = 512
    kern = functools.partial(_outproj_ln_kernel, alpha=alpha, split=split)
    row = lambda width: pl.BlockSpec((bm, width), lambda i: (i, 0))
    full = lambda shape: pl.BlockSpec(shape, lambda i: (0, 0))
    return pl.pallas_call(
        kern,
        grid=(m // bm,),
        in_specs=[row(split), row(ret.shape[1]), full(wo.shape), row(d), full((1, d)), full((1, d))],
        out_specs=[row(d), row(d)],
        out_shape=[jax.ShapeDtypeStruct((m, d), F32), jax.ShapeDtypeStruct((m, d), BF16)],
        compiler_params=_params(("parallel",)),
        name="outproj_ln",
    )(sb, ret, wo, x, lw.reshape(1, d), lb.reshape(1, d))


def _top_ranks(s, iota, k, exact_ties):
    n = s.shape[0]
    work = s
    rank = jnp.full(s.shape, float(k), F32)
    tops = []
    for r in range(k):
        m = jnp.max(work, axis=0, keepdims=True)
        sel = work == m
        if exact_ties:
            first = jnp.min(jnp.where(sel, iota, float(n)), axis=0, keepdims=True)
            sel = iota == first
        rank = jnp.where(sel, float(r), rank)
        work = jnp.where(sel, -jnp.inf, work)
        tops.append(m)
    return rank, tops


def _route_column(s1, s2, iota, ciota, widths, exact_ties):
    k = PEER_TOPK
    nk, lanes = s1.shape
    n_cand = sum(widths)
    n_pad = ciota.shape[0]
    rank1, top1 = _top_ranks(s1, iota, k, exact_ties)
    rank2, top2 = _top_ranks(s2, iota, k, exact_ties)
    top2_all = jnp.concatenate(top2, axis=0)
    cand = [top1[a] + top2_all[:widths[a]] for a in range(k)]
    if n_pad > n_cand:
        cand.append(jnp.full((n_pad - n_cand, lanes), -jnp.inf, F32))
    cand = jnp.concatenate(cand, axis=0)
    crank, _ = _top_ranks(cand, ciota, k, exact_ties)
    chosen = crank < float(k)
    best = top1[0] + top2[0]
    z = jnp.sum(jnp.where(chosen, jnp.exp(cand - best), 0.0), axis=0, keepdims=True)
    ones = jnp.where(chosen, 1.0, 0.0)
    cnt = jnp.zeros((nk, lanes), F32)
    off = 0
    for a in range(k):
        cnt_a = jnp.sum(ones[off:off + widths[a]], axis=0, keepdims=True)
        cnt = jnp.where(rank1 == float(a), cnt_a, cnt)
        off += widths[a]
    picked = (jnp.sum(jnp.where(rank1 < float(k), 1.0, 0.0), axis=0, keepdims=True)
              + jnp.sum(jnp.where(rank2 < float(k), 1.0, 0.0), axis=0, keepdims=True)
              + jnp.sum(ones, axis=0, keepdims=True))
    unique = picked == float(3 * k)
    return (cnt, jnp.exp(s1 - top1[0]), rank2, jnp.exp(s2 - top2[0]) / z), unique


def _router_kernel(q_ref, keys_ref, cnt_ref, e1_ref, rank2_ref, e2_ref, *, lanes, cols_per_iter):
    nk = PEER_KEYS
    k = PEER_TOPK
    iota = lax.broadcasted_iota(jnp.int32, (nk, lanes), 0).astype(F32)
    widths = [k // (a + 1) for a in range(k)]
    n_pad = -(-sum(widths) // F32_SUBLANES) * F32_SUBLANES
    ciota = lax.broadcasted_iota(jnp.int32, (n_pad, lanes), 0).astype(F32)
    out_refs = (cnt_ref, e1_ref, rank2_ref, e2_ref)

    def columns(group, _):
        pending = []
        for j in range(cols_per_iter):
            c = group * cols_per_iter + j
            sl = pl.ds(pl.multiple_of(c * lanes, lanes), lanes)
            q = q_ref[sl, :]
            half = q.shape[1] // 2
            s1 = lax.dot_general(keys_ref[0, 0], q[:, :half], NT_DIMS, preferred_element_type=F32)
            s2 = lax.dot_general(keys_ref[0, 1], q[:, half:], NT_DIMS, preferred_element_type=F32)
            outs, unique = _route_column(s1, s2, iota, ciota, widths, exact_ties=False)
            for ref, val in zip(out_refs, outs):
                ref[0, c] = val
            pending.append((c, s1, s2, jnp.min(jnp.where(unique, 1.0, 0.0))))
        for c, s1, s2, all_unique in pending:
            @pl.when(all_unique < 0.5)
            def _():
                exact, _ = _route_column(s1, s2, iota, ciota, widths, exact_ties=True)
                for ref, val in zip(out_refs, exact):
                    ref[0, c] = val
        return 0

    lax.fori_loop(0, q_ref.shape[0] // (lanes * cols_per_iter), columns, 0)


def _router(qp, sub_keys, tb):
    t = qp.shape[0]
    h, _, nk, half = sub_keys.shape
    out_f32 = jax.ShapeDtypeStruct((h, t // LANES, nk, LANES), F32)
    out_spec = pl.BlockSpec((1, tb // LANES, nk, LANES), lambda j, hh: (hh, j, 0, 0))
    return pl.pallas_call(
        functools.partial(_router_kernel, lanes=LANES, cols_per_iter=4),
        grid=(t // tb, h),
        in_specs=[pl.BlockSpec((tb, 2 * half), lambda j, hh: (j, hh)),
                  pl.BlockSpec((1, 2, nk, half), lambda j, hh: (hh, 0, 0, 0))],
        out_specs=[out_spec, out_spec, out_spec, out_spec],
        out_shape=[out_f32, out_f32, out_f32, out_f32],
        compiler_params=_params(("parallel", "parallel")),
        name="peer_router",
    )(qp, sub_keys)


def _row_to_packed_tile(ref, h, c, row, n_rows):
    rep = jnp.broadcast_to(ref[h, c, 0, row:row + 1, :], (BF16_SUBLANES, LANES))
    packed = rep.astype(BF16)
    return jnp.concatenate([packed] * (n_rows // BF16_SUBLANES), axis=0)


def _peer_kernel(xb_ref, u_ref, v_ref, cnt_ref, e1_ref, rank2_in, e2_in, x_ref, lw_ref, lb_ref, o_ref,
                 act_ref, hid_ref, rank2_ref, e2_ref, *, n_sub, alpha):
    e = pl.program_id(1)
    tb = xb_ref.shape[0]
    nk = PEER_KEYS
    n_col = tb // LANES

    @pl.when(e == 0)
    def _():
        rank2_ref[...] = rank2_in[...].astype(BF16)
        e2_ref[...] = e2_in[...].astype(BF16)
        o_ref[...] = jnp.zeros_like(o_ref)

    act_ref[...] = lax.dot_general(u_ref[...], xb_ref[...], NT_DIMS, preferred_element_type=F32)
    for ii in range(n_sub):
        for c in range(n_col):
            gate = jnp.zeros((nk, LANES), BF16)
            for h in range(PEER_HEADS):
                cnt = _row_to_packed_tile(cnt_ref, h, c, ii, nk)
                e1 = _row_to_packed_tile(e1_ref, h, c, ii, nk)
                gate = gate + jnp.where(rank2_ref[h, c] < cnt, e2_ref[h, c] * e1, jnp.zeros((), BF16))
            a = act_ref[ii * nk:(ii + 1) * nk, c * LANES:(c + 1) * LANES]
            gelu = 0.5 * a * (1.0 + lax.erf(a * (2.0 ** -0.5)))
            hid_ref[c * LANES:(c + 1) * LANES, ii * nk:(ii + 1) * nk] = (gelu.astype(BF16) * gate).T
    o_ref[...] += jnp.dot(hid_ref[...], v_ref[...], preferred_element_type=F32)

    @pl.when(e == pl.num_programs(1) - 1)
    def _():
        o_ref[...] = _layer_norm(alpha * x_ref[...] + o_ref[...], lw_ref[...], lb_ref[...])


def _peer_ffn_ln(xb, x, u, v, cnt, e1, rank2, e2, lw, lb, alpha, tb, eb):
    t, d = xb.shape
    n_exp = u.shape[0]
    h, n_col_all, nk, _ = cnt.shape
    assert eb == F32_SUBLANES * nk
    tokens = pl.BlockSpec((tb, d), lambda j, e: (j, 0))
    table = pl.BlockSpec((eb, d), lambda j, e: (e, 0))
    route = pl.BlockSpec((h, tb // LANES, nk, LANES), lambda j, e: (0, j, 0, 0))
    first_key_rows = pl.BlockSpec((h, tb // LANES, 1, F32_SUBLANES, LANES), lambda j, e: (0, j, e, 0, 0))
    vector = pl.BlockSpec((1, d), lambda j, e: (0, 0))
    cnt, e1 = (a.reshape(h, n_col_all, nk // F32_SUBLANES, F32_SUBLANES, LANES) for a in (cnt, e1))
    packed = pltpu.VMEM((h, tb // LANES, nk, LANES), BF16)
    return pl.pallas_call(
        functools.partial(_peer_kernel, n_sub=eb // nk, alpha=alpha),
        grid=(t // tb, n_exp // eb),
        in_specs=[tokens, table, table, first_key_rows, first_key_rows, route, route, tokens, vector, vector],
        out_specs=tokens,
        out_shape=jax.ShapeDtypeStruct((t, d), F32),
        scratch_shapes=[pltpu.VMEM((eb, tb), F32), pltpu.VMEM((tb, eb), BF16), packed, packed],
        compiler_params=_params(("parallel", "arbitrary")),
        name="peer_ffn_ln",
    )(xb, u, v, cnt, e1, rank2, e2, x, lw.reshape(1, d), lb.reshape(1, d))


def _layer(x, w_in, sb_norm_w, ret_norm_w, w_out, ln1_w, ln1_b, w_pq, sub_keys, expert_u, expert_v,
           ln2_w, ln2_b, alpha):
    b, s, d = x.shape
    n_tok = b * s
    sb_heads = sb_norm_w.shape[0] // HEAD_DIM
    ret_heads = ret_norm_w.shape[0] // HEAD_DIM
    xt = x.reshape(n_tok, d)
    col_scale = jnp.where(jnp.arange(w_in.shape[1]) < sb_heads * HEAD_DIM, SB_Q_SCALE, 1.0).astype(F32)
    proj = _matmul(xt.astype(BF16), w_in, 1024, 1024, BF16, col_scale).reshape(b, s, -1)
    sb = _sb_attention(proj, sb_norm_w, sb_heads, 0)
    ret = _retention(proj, ret_norm_w, ret_heads, 3 * sb_heads)
    x1, x1b = _outproj_ln(sb.reshape(n_tok, -1), ret.reshape(n_tok, -1), w_out.astype(BF16), xt,
                          ln1_w, ln1_b, alpha)
    qp = _matmul(x1b, w_pq, 1024, 1024, BF16)
    cnt, e1, rank2, e2 = _router(qp, sub_keys.astype(BF16), 512)
    out = _peer_ffn_ln(x1b, x1, expert_u.astype(BF16), expert_v.astype(BF16), cnt, e1, rank2, e2, ln2_w, ln2_b,
                       alpha, 512, 1024)
    return out.reshape(b, s, d)


def kernel(x, w_in, sb_norm_w, ret_norm_w, w_out, ln1_w, ln1_b, w_pq, sub_keys, expert_u, expert_v, ln2_w, ln2_b):
    depth = w_in.shape[0]
    alpha = (2.0 * depth) ** 0.25
    for layer in range(depth):
        x = _layer(x, w_in[layer], sb_norm_w[layer], ret_norm_w[layer], w_out[layer], ln1_w[layer],
                   ln1_b[layer], w_pq[layer], sub_keys[layer], expert_u[layer], expert_v[layer],
                   ln2_w[layer], ln2_b[layer], alpha)
    return x
```

```python
import functools
import math

import jax
import jax.numpy as jnp
from jax import lax
from jax.experimental import pallas as pl
from jax.experimental.pallas import tpu as pltpu

F32 = jnp.float32
BF16 = jnp.bfloat16

LANES = 128
F32_SUBLANES = 8
BF16_SUBLANES = 16
HEAD_DIM = 128
CHUNK = 64
ROPE_THETA = 10000.0
PEER_HEADS = 8
PEER_KEYS = 128
PEER_TOPK = 16
LN_EPS = 1e-5
NORM_EPS = 1e-6
SB_Q_SCALE = HEAD_DIM ** -0.5 * math.log2(math.e)
F32_EXP2_UNDERFLOW = 160.0

V7X_VMEM_BYTES = 64 * 1024 * 1024
VMEM_LIMIT = V7X_VMEM_BYTES - 6 * 1024 * 1024

NT_DIMS = (((1,), (1,)), ((), ()))
TN_DIMS = (((0,), (0,)), ((), ()))


def _params(semantics):
    return pltpu.CompilerParams(dimension_semantics=semantics, vmem_limit_bytes=VMEM_LIMIT)


def _matmul_kernel(x_ref, w_ref, *rest, scaled):
    scale_ref = rest[0] if scaled else None
    o_ref, wb_ref = rest[-2:]

    @pl.when(pl.program_id(1) == 0)
    def _():
        w = w_ref[...] * scale_ref[...] if scaled else w_ref[...]
        wb_ref[...] = w.astype(BF16)

    o_ref[...] = jnp.dot(x_ref[...], wb_ref[...], preferred_element_type=F32).astype(o_ref.dtype)


def _matmul(x, w, bm, bn, out_dtype, col_scale=None):
    m, k = x.shape
    n = w.shape[1]
    scaled = col_scale is not None
    in_specs = [pl.BlockSpec((bm, k), lambda j, i: (i, 0)), pl.BlockSpec((k, bn), lambda j, i: (0, j))]
    args = [x, w]
    if scaled:
        in_specs.append(pl.BlockSpec((1, bn), lambda j, i: (0, j)))
        args.append(col_scale.reshape(1, n))
    return pl.pallas_call(
        functools.partial(_matmul_kernel, scaled=scaled),
        grid=(n // bn, m // bm),
        in_specs=in_specs,
        out_specs=pl.BlockSpec((bm, bn), lambda j, i: (i, j)),
        out_shape=jax.ShapeDtypeStruct((m, n), out_dtype),
        scratch_shapes=[pltpu.VMEM((k, bn), BF16)],
        compiler_params=_params(("parallel", "arbitrary")),
        name="dense_proj",
    )(*args)


def _sb_kernel(q_ref, k_ref, v_ref, tri_ref, w_ref, o_ref, acc_ref, run_ref, *, t, n_group):
    qi = pl.program_id(2)
    tri = tri_ref[...]
    below_diag = (lax.broadcasted_iota(jnp.int32, (t, t), 1) < lax.broadcasted_iota(jnp.int32, (t, t), 0))
    lanes = [slice(g * HEAD_DIM, (g + 1) * HEAD_DIM) for g in range(n_group)]

    def block(j, on_diagonal):
        start = pl.multiple_of(j * t, t)

        def scores(g):
            return lax.dot_general(q_ref[0, :, lanes[g]], k_ref[0, pl.ds(start, t), lanes[g]], NT_DIMS,
                                   preferred_element_type=F32)

        def suffix_sums(z):
            softplus = jnp.maximum(z, 0.0) + jnp.log2(1.0 + jnp.exp2(-jnp.abs(z)))
            log_beta = z - softplus
            if on_diagonal:
                softplus = jnp.where(below_diag, softplus, 0.0)
            hi = softplus.astype(BF16)
            lo = (softplus - hi.astype(F32)).astype(BF16)
            suffix = jnp.dot(jnp.concatenate([hi, lo], axis=1), tri, preferred_element_type=F32)
            return log_beta, suffix, jnp.sum(softplus, axis=1, keepdims=True)

        def weighted_values(g, log_beta, suffix, total):
            vs = v_ref[0, pl.ds(start, t), lanes[g]]
            if on_diagonal:
                a = jnp.where(below_diag, jnp.exp2(log_beta - suffix), 0.0)
                acc_ref[:, lanes[g]] = jnp.dot(a.astype(BF16), vs, preferred_element_type=F32)
                run_ref[g] = total
            else:
                run = run_ref[g]
                a = jnp.exp2(log_beta - suffix - run)
                acc_ref[:, lanes[g]] += jnp.dot(a.astype(BF16), vs, preferred_element_type=F32)
                run_ref[g] = run + total

        z, mid = {}, {}
        for step in range(n_group + 2):
            if step < n_group:
                z[step] = scores(step)
            if 0 <= step - 1 < n_group:
                mid[step - 1] = suffix_sums(z.pop(step - 1))
            if 0 <= step - 2 < n_group:
                weighted_values(step - 2, *mid.pop(step - 2))

    def smallest_run():
        m = run_ref[0]
        for g in range(1, n_group):
            m = jnp.minimum(m, run_ref[g])
        return jnp.min(m)

    block(qi, True)

    def more_to_do(state):
        jj, low = state
        return jnp.logical_and(jj <= qi, low < F32_EXP2_UNDERFLOW)

    def off_diagonal(state):
        jj, _ = state
        block(qi - jj, False)
        return jj + 1, smallest_run()

    lax.while_loop(more_to_do, off_diagonal, (jnp.int32(1), smallest_run()))
    for g in range(n_group):
        acc = acc_ref[:, lanes[g]]
        ms = jnp.mean(acc * acc, axis=-1, keepdims=True)
        o_ref[0, :, lanes[g]] = (acc * lax.rsqrt(ms + NORM_EPS) * w_ref[:, lanes[g]]).astype(o_ref.dtype)


def _sb_attention(proj, norm_w, n_heads, col0, n_group=8, t=256):
    b, s, _ = proj.shape
    width = n_group * HEAD_DIM
    strict_lower = (jnp.arange(t)[:, None] > jnp.arange(t)[None, :]).astype(BF16)
    tri = jnp.concatenate([strict_lower, strict_lower], axis=0)
    kern = functools.partial(_sb_kernel, t=t, n_group=n_group)
    group_col = lambda part: (col0 + part * n_heads) // n_group
    return pl.pallas_call(
        kern,
        grid=(b, n_heads // n_group, s // t),
        in_specs=[pl.BlockSpec((1, t, width), lambda bi, hg, i: (bi, i, group_col(0) + hg)),
                  pl.BlockSpec((1, s, width), lambda bi, hg, i: (bi, 0, group_col(1) + hg)),
                  pl.BlockSpec((1, s, width), lambda bi, hg, i: (bi, 0, group_col(2) + hg)),
                  pl.BlockSpec((2 * t, t), lambda bi, hg, i: (0, 0)),
                  pl.BlockSpec((1, width), lambda bi, hg, i: (0, hg))],
        out_specs=pl.BlockSpec((1, t, width), lambda bi, hg, i: (bi, i, hg)),
        out_shape=jax.ShapeDtypeStruct((b, s, n_heads * HEAD_DIM), BF16),
        scratch_shapes=[pltpu.VMEM((t, width), F32), pltpu.VMEM((n_group, t, 1), F32)],
        compiler_params=_params(("parallel", "parallel", "arbitrary")),
        name="sb_attention",
    )(proj, proj, proj, tri, norm_w.reshape(1, -1).astype(F32))


def _ret_kernel(q_ref, k_ref, v_ref, g_ref, cos_ref, sin_ref, dec_ref, xi_ref, zeta_ref, gt_ref, w_ref,
                o_ref, state_ref, *, n_group):
    @pl.when(pl.program_id(2) == 0)
    def _():
        state_ref[...] = jnp.zeros_like(state_ref)

    cos = cos_ref[...]
    sin = sin_ref[...]
    half = HEAD_DIM // 2
    lanes = [slice(g * HEAD_DIM, (g + 1) * HEAD_DIM) for g in range(n_group)]

    def rotate_and_score(g):
        q = q_ref[0, :, lanes[g]].astype(F32)
        k = k_ref[0, :, lanes[g]].astype(F32)
        qr = q * cos + pltpu.roll(q, half, 1) * sin
        kr = k * cos + pltpu.roll(k, half, 1) * sin
        qb = qr.astype(BF16)
        scores = lax.dot_general(qb, kr.astype(BF16), NT_DIMS, preferred_element_type=F32)
        carried = jnp.dot(qb, state_ref[g].astype(BF16), preferred_element_type=F32)
        return kr, scores, carried

    def mix_values(g, kr, scores, carried):
        v = v_ref[0, :, lanes[g]]
        ret = jnp.dot((scores * dec_ref[g]).astype(BF16), v, preferred_element_type=F32) + carried * xi_ref[g]
        kz = (kr * zeta_ref[g]).astype(BF16)
        state_ref[g] = state_ref[g] * gt_ref[g] + lax.dot_general(kz, v, TN_DIMS, preferred_element_type=F32)
        return ret

    def normalise(g, ret):
        mu = jnp.mean(ret, axis=-1, keepdims=True)
        cen = ret - mu
        var = jnp.mean(cen * cen, axis=-1, keepdims=True)
        gate = g_ref[0, :, lanes[g]].astype(F32)
        silu = gate / (1.0 + jnp.exp(-gate))
        o_ref[0, :, lanes[g]] = (cen * lax.rsqrt(var + NORM_EPS) * w_ref[:, lanes[g]] * silu).astype(o_ref.dtype)

    first, second = {}, {}
    for step in range(n_group + 2):
        if step < n_group:
            first[step] = rotate_and_score(step)
        if 0 <= step - 1 < n_group:
            second[step - 1] = mix_values(step - 1, *first.pop(step - 1))
        if 0 <= step - 2 < n_group:
            normalise(step - 2, second.pop(step - 2))


def _retention(proj, norm_w, n_heads, col0, n_group=4):
    b, s, _ = proj.shape
    t = 256
    d = HEAD_DIM
    inv_freq = ROPE_THETA ** (-jnp.arange(0, d, 2, dtype=F32) / d)
    ang = jnp.arange(s, dtype=F32)[:, None] * inv_freq[None, :]
    ang = jnp.concatenate([ang, ang], -1)
    sign = jnp.concatenate([-jnp.ones((d // 2,), F32), jnp.ones((d // 2,), F32)])
    cos_t = jnp.cos(ang)
    sin_t = jnp.sin(ang) * sign[None, :]
    log_g = jnp.log1p(-(2.0 ** (-5.0 - jnp.arange(n_heads, dtype=F32))))
    idx = jnp.arange(t, dtype=F32)
    same_or_earlier_chunk = (jnp.floor(idx[None, :] / CHUNK) <= jnp.floor(idx[:, None] / CHUNK))
    dec = jnp.exp(log_g[:, None, None] * jnp.abs(idx[:, None] - idx[None, :])) * (d ** -0.5)
    dec = jnp.where(same_or_earlier_chunk[None], dec, 0.0)
    xi = jnp.broadcast_to(jnp.exp(log_g[:, None] * (idx + 1.0))[:, :, None], (n_heads, t, d))
    zeta = jnp.broadcast_to((jnp.exp(log_g[:, None] * (t - 1.0 - idx)) * (d ** -0.5))[:, :, None], (n_heads, t, d))
    gt = jnp.broadcast_to(jnp.exp(log_g * t)[:, None, None], (n_heads, 1, d))
    width = n_group * d
    group_spec = lambda part: pl.BlockSpec(
        (1, t, width), lambda bi, hg, i: (bi, i, (col0 + part * n_heads) // n_group + hg))
    per_head = lambda rows, cols: pl.BlockSpec((n_group, rows, cols), lambda bi, hg, i: (hg, 0, 0))
    return pl.pallas_call(
        functools.partial(_ret_kernel, n_group=n_group),
        grid=(b, n_heads // n_group, s // t),
        in_specs=[group_spec(0), group_spec(1), group_spec(2), group_spec(3),
                  pl.BlockSpec((t, d), lambda bi, hg, i: (i, 0)),
                  pl.BlockSpec((t, d), lambda bi, hg, i: (i, 0)),
                  per_head(t, t), per_head(t, d), per_head(t, d), per_head(1, d),
                  pl.BlockSpec((1, width), lambda bi, hg, i: (0, hg))],
        out_specs=pl.BlockSpec((1, t, width), lambda bi, hg, i: (bi, i, hg)),
        out_shape=jax.ShapeDtypeStruct((b, s, n_heads * d), BF16),
        scratch_shapes=[pltpu.VMEM((n_group, d, d), F32)],
        compiler_params=_params(("parallel", "parallel", "arbitrary")),
        name="retention",
    )(proj, proj, proj, proj, cos_t, sin_t, dec, xi, zeta, gt, norm_w.reshape(1, -1).astype(F32))


def _layer_norm(y, w, b):
    mu = jnp.mean(y, axis=-1, keepdims=True)
    cen = y - mu
    var = jnp.mean(cen * cen, axis=-1, keepdims=True)
    return cen * lax.rsqrt(var + LN_EPS) * w + b


def _outproj_ln_kernel(sb_ref, ret_ref, wo_ref, x_ref, lw_ref, lb_ref, o_ref, ob_ref, *, alpha, split):
    mix = jnp.dot(sb_ref[...], wo_ref[:split, :], preferred_element_type=F32)
    mix = mix + jnp.dot(ret_ref[...], wo_ref[split:, :], preferred_element_type=F32)
    y = _layer_norm(alpha * x_ref[...] + mix, lw_ref[...], lb_ref[...])
    o_ref[...] = y
    ob_ref[...] = y.astype(BF16)


def _outproj_ln(sb, ret, wo, x, lw, lb, alpha):
    m, d = x.shape
    split = sb.shape[1]
    bm = 512
    kern = functools.partial(_outproj_ln_kernel, alpha=alpha, split=split)
    row = lambda width: pl.BlockSpec((bm, width), lambda i: (i, 0))
    full = lambda shape: pl.BlockSpec(shape, lambda i: (0, 0))
    return pl.pallas_call(
        kern,
        grid=(m // bm,),
        in_specs=[row(split), row(ret.shape[1]), full(wo.shape), row(d), full((1, d)), full((1, d))],
        out_specs=[row(d), row(d)],
        out_shape=[jax.ShapeDtypeStruct((m, d), F32), jax.ShapeDtypeStruct((m, d), BF16)],
        compiler_params=_params(("parallel",)),
        name="outproj_ln",
    )(sb, ret, wo, x, lw.reshape(1, d), lb.reshape(1, d))


def _top_ranks(s, iota, k, exact_ties):
    n = s.shape[0]
    work = s
    rank = jnp.full(s.shape, float(k), F32)
    tops = []
    for r in range(k):
        m = jnp.max(work, axis=0, keepdims=True)
        sel = work == m
        if exact_ties:
            first = jnp.min(jnp.where(sel, iota, float(n)), axis=0, keepdims=True)
            sel = iota == first
        rank = jnp.where(sel, float(r), rank)
        work = jnp.where(sel, -jnp.inf, work)
        tops.append(m)
    return rank, tops


def _route_column(s1, s2, iota, ciota, widths, exact_ties):
    k = PEER_TOPK
    nk, lanes = s1.shape
    n_cand = sum(widths)
    n_pad = ciota.shape[0]
    rank1, top1 = _top_ranks(s1, iota, k, exact_ties)
    rank2, top2 = _top_ranks(s2, iota, k, exact_ties)
    top2_all = jnp.concatenate(top2, axis=0)
    cand = [top1[a] + top2_all[:widths[a]] for a in range(k)]
    if n_pad > n_cand:
        cand.append(jnp.full((n_pad - n_cand, lanes), -jnp.inf, F32))
    cand = jnp.concatenate(cand, axis=0)
    crank, _ = _top_ranks(cand, ciota, k, exact_ties)
    chosen = crank < float(k)
    best = top1[0] + top2[0]
    z = jnp.sum(jnp.where(chosen, jnp.exp(cand - best), 0.0), axis=0, keepdims=True)
    ones = jnp.where(chosen, 1.0, 0.0)
    cnt = jnp.zeros((nk, lanes), F32)
    off = 0
    for a in range(k):
        cnt_a = jnp.sum(ones[off:off + widths[a]], axis=0, keepdims=True)
        cnt = jnp.where(rank1 == float(a), cnt_a, cnt)
        off += widths[a]
    picked = (jnp.sum(jnp.where(rank1 < float(k), 1.0, 0.0), axis=0, keepdims=True)
              + jnp.sum(jnp.where(rank2 < float(k), 1.0, 0.0), axis=0, keepdims=True)
              + jnp.sum(ones, axis=0, keepdims=True))
    unique = picked == float(3 * k)
    return (cnt, jnp.exp(s1 - top1[0]), rank2, jnp.exp(s2 - top2[0]) / z), unique


def _router_kernel(q_ref, keys_ref, cnt_ref, e1_ref, rank2_ref, e2_ref, *, lanes, cols_per_iter):
    nk = PEER_KEYS
    k = PEER_TOPK
    iota = lax.broadcasted_iota(jnp.int32, (nk, lanes), 0).astype(F32)
    widths = [k // (a + 1) for a in range(k)]
    n_pad = -(-sum(widths) // F32_SUBLANES) * F32_SUBLANES
    ciota = lax.broadcasted_iota(jnp.int32, (n_pad, lanes), 0).astype(F32)
    out_refs = (cnt_ref, e1_ref, rank2_ref, e2_ref)

    def columns(group, _):
        pending = []
        for j in range(cols_per_iter):
            c = group * cols_per_iter + j
            sl = pl.ds(pl.multiple_of(c * lanes, lanes), lanes)
            q = q_ref[sl, :]
            half = q.shape[1] // 2
            s1 = lax.dot_general(keys_ref[0, 0], q[:, :half], NT_DIMS, preferred_element_type=F32)
            s2 = lax.dot_general(keys_ref[0, 1], q[:, half:], NT_DIMS, preferred_element_type=F32)
            outs, unique = _route_column(s1, s2, iota, ciota, widths, exact_ties=False)
            for ref, val in zip(out_refs, outs):
                ref[0, c] = val
            pending.append((c, s1, s2, jnp.min(jnp.where(unique, 1.0, 0.0))))
        for c, s1, s2, all_unique in pending:
            @pl.when(all_unique < 0.5)
            def _():
                exact, _ = _route_column(s1, s2, iota, ciota, widths, exact_ties=True)
                for ref, val in zip(out_refs, exact):
                    ref[0, c] = val
        return 0

    lax.fori_loop(0, q_ref.shape[0] // (lanes * cols_per_iter), columns, 0)


def _router(qp, sub_keys, tb):
    t = qp.shape[0]
    h, _, nk, half = sub_keys.shape
    out_f32 = jax.ShapeDtypeStruct((h, t // LANES, nk, LANES), F32)
    out_spec = pl.BlockSpec((1, tb // LANES, nk, LANES), lambda j, hh: (hh, j, 0, 0))
    return pl.pallas_call(
        functools.partial(_router_kernel, lanes=LANES, cols_per_iter=4),
        grid=(t // tb, h),
        in_specs=[pl.BlockSpec((tb, 2 * half), lambda j, hh: (j, hh)),
                  pl.BlockSpec((1, 2, nk, half), lambda j, hh: (hh, 0, 0, 0))],
        out_specs=[out_spec, out_spec, out_spec, out_spec],
        out_shape=[out_f32, out_f32, out_f32, out_f32],
        compiler_params=_params(("parallel", "parallel")),
        name="peer_router",
    )(qp, sub_keys)


def _row_to_packed_tile(ref, h, c, row, n_rows):
    rep = jnp.broadcast_to(ref[h, c, 0, row:row + 1, :], (BF16_SUBLANES, LANES))
    packed = rep.astype(BF16)
    return jnp.concatenate([packed] * (n_rows // BF16_SUBLANES), axis=0)


def _peer_kernel(u_ref, v_ref, cnt_ref, e1_ref, rank2_in, e2_in, x_ref, lw_ref, lb_ref, o_ref,
                 act_ref, hid_ref, rank2_ref, e2_ref, xb_ref, *, n_sub, alpha):
    e = pl.program_id(1)
    tb = x_ref.shape[0]
    nk = PEER_KEYS
    n_col = tb // LANES

    @pl.when(e == 0)
    def _():
        rank2_ref[...] = rank2_in[...].astype(BF16)
        e2_ref[...] = e2_in[...].astype(BF16)
        o_ref[...] = jnp.zeros_like(o_ref)
        xb_ref[...] = x_ref[...].astype(BF16)

    act_ref[...] = lax.dot_general(pltpu.bitcast(u_ref[...], BF16), xb_ref[...], NT_DIMS,
                                   preferred_element_type=F32)
    for ii in range(n_sub):
        for c in range(n_col):
            gate = jnp.zeros((nk, LANES), BF16)
            for h in range(PEER_HEADS):
                cnt = _row_to_packed_tile(cnt_ref, h, c, ii, nk)
                e1 = _row_to_packed_tile(e1_ref, h, c, ii, nk)
                gate = gate + jnp.where(rank2_ref[h, c] < cnt, e2_ref[h, c] * e1, jnp.zeros((), BF16))
            a = act_ref[ii * nk:(ii + 1) * nk, c * LANES:(c + 1) * LANES]
            gelu = 0.5 * a * (1.0 + lax.erf(a * (2.0 ** -0.5)))
            hid_ref[c * LANES:(c + 1) * LANES, ii * nk:(ii + 1) * nk] = (gelu.astype(BF16) * gate).T
    o_ref[...] += jnp.dot(hid_ref[...], pltpu.bitcast(v_ref[...], BF16), preferred_element_type=F32)

    @pl.when(e == pl.num_programs(1) - 1)
    def _():
        o_ref[...] = _layer_norm(alpha * x_ref[...] + o_ref[...], lw_ref[...], lb_ref[...])


def _pack_row_pairs(a):
    r, c = a.shape
    return lax.bitcast_convert_type(a.reshape(r // 2, 2, c).swapaxes(1, 2), jnp.uint32)


def _peer_ffn_ln(x, u, v, cnt, e1, rank2, e2, lw, lb, alpha, tb, eb):
    t, d = x.shape
    n_exp = u.shape[0]
    h, n_col_all, nk, _ = cnt.shape
    assert eb == F32_SUBLANES * nk
    tokens = pl.BlockSpec((tb, d), lambda j, e: (j, 0))
    table = pl.BlockSpec((eb // 2, d), lambda j, e: (e, 0))
    route = pl.BlockSpec((h, tb // LANES, nk, LANES), lambda j, e: (0, j, 0, 0))
    first_key_rows = pl.BlockSpec((h, tb // LANES, 1, F32_SUBLANES, LANES), lambda j, e: (0, j, e, 0, 0))
    vector = pl.BlockSpec((1, d), lambda j, e: (0, 0))
    cnt, e1 = (a.reshape(h, n_col_all, nk // F32_SUBLANES, F32_SUBLANES, LANES) for a in (cnt, e1))
    packed = pltpu.VMEM((h, tb // LANES, nk, LANES), BF16)
    return pl.pallas_call(
        functools.partial(_peer_kernel, n_sub=eb // nk, alpha=alpha),
        grid=(t // tb, n_exp // eb),
        in_specs=[table, table, first_key_rows, first_key_rows, route, route, tokens, vector, vector],
        out_specs=tokens,
        out_shape=jax.ShapeDtypeStruct((t, d), F32),
        scratch_shapes=[pltpu.VMEM((eb, tb), F32), pltpu.VMEM((tb, eb), BF16), packed, packed,
                        pltpu.VMEM((tb, d), BF16)],
        compiler_params=_params(("parallel", "arbitrary")),
        name="peer_ffn_ln",
    )(_pack_row_pairs(u), _pack_row_pairs(v), cnt, e1, rank2, e2, x, lw.reshape(1, d), lb.reshape(1, d))


def _layer(x, w_in, sb_norm_w, ret_norm_w, w_out, ln1_w, ln1_b, w_pq, sub_keys, expert_u, expert_v,
           ln2_w, ln2_b, alpha):
    b, s, d = x.shape
    n_tok = b * s
    sb_heads = sb_norm_w.shape[0] // HEAD_DIM
    ret_heads = ret_norm_w.shape[0] // HEAD_DIM
    xt = x.reshape(n_tok, d)
    col_scale = jnp.where(jnp.arange(w_in.shape[1]) < sb_heads * HEAD_DIM, SB_Q_SCALE, 1.0).astype(F32)
    proj = _matmul(xt.astype(BF16), w_in, 1024, 1024, BF16, col_scale).reshape(b, s, -1)
    sb = _sb_attention(proj, sb_norm_w, sb_heads, 0)
    ret = _retention(proj, ret_norm_w, ret_heads, 3 * sb_heads)
    x1, x1b = _outproj_ln(sb.reshape(n_tok, -1), ret.reshape(n_tok, -1), w_out.astype(BF16), xt,
                          ln1_w, ln1_b, alpha)
    qp = _matmul(x1b, w_pq, 1024, 1024, BF16)
    cnt, e1, rank2, e2 = _router(qp, sub_keys.astype(BF16), 512)
    out = _peer_ffn_ln(x1, expert_u.astype(BF16), expert_v.astype(BF16), cnt, e1, rank2, e2, ln2_w, ln2_b,
                       alpha, 512, 1024)
    return out.reshape(b, s, d)


def kernel(x, w_in, sb_norm_w, ret_norm_w, w_out, ln1_w, ln1_b, w_pq, sub_keys, expert_u, expert_v, ln2_w, ln2_b):
    depth = w_in.shape[0]
    alpha = (2.0 * depth) ** 0.25
    for layer in range(depth):
        x = _layer(x, w_in[layer], sb_norm_w[layer], ret_norm_w[layer], w_out[layer], ln1_w[layer],
                   ln1_b[layer], w_pq[layer], sub_keys[layer], expert_u[layer], expert_v[layer],
                   ln2_w[layer], ln2_b[layer], alpha)
    return x
```

```python
import functools
import math

import jax
import jax.numpy as jnp
from jax import lax
from jax.experimental import pallas as pl
from jax.experimental.pallas import tpu as pltpu

F32 = jnp.float32
BF16 = jnp.bfloat16

LANES = 128
F32_SUBLANES = 8
BF16_SUBLANES = 16
HEAD_DIM = 128
CHUNK = 64
ROPE_THETA = 10000.0
PEER_HEADS = 8
PEER_KEYS = 128
PEER_TOPK = 16
LN_EPS = 1e-5
NORM_EPS = 1e-6
SB_Q_SCALE = HEAD_DIM ** -0.5 * math.log2(math.e)
F32_EXP2_UNDERFLOW = 160.0

V7X_VMEM_BYTES = 64 * 1024 * 1024
VMEM_LIMIT = V7X_VMEM_BYTES - 6 * 1024 * 1024

NT_DIMS = (((1,), (1,)), ((), ()))
TN_DIMS = (((0,), (0,)), ((), ()))


def _params(semantics):
    return pltpu.CompilerParams(dimension_semantics=semantics, vmem_limit_bytes=VMEM_LIMIT)


def _matmul_kernel(x_ref, w_ref, *rest, scaled):
    scale_ref = rest[0] if scaled else None
    o_ref, wb_ref = rest[-2:]

    @pl.when(pl.program_id(1) == 0)
    def _():
        w = w_ref[...] * scale_ref[...] if scaled else w_ref[...]
        wb_ref[...] = w.astype(BF16)

    o_ref[...] = jnp.dot(x_ref[...].astype(BF16), wb_ref[...], preferred_element_type=F32).astype(o_ref.dtype)


def _matmul(x, w, bm, bn, out_dtype, col_scale=None):
    m, k = x.shape
    n = w.shape[1]
    scaled = col_scale is not None
    in_specs = [pl.BlockSpec((bm, k), lambda j, i: (i, 0)), pl.BlockSpec((k, bn), lambda j, i: (0, j))]
    args = [x, w]
    if scaled:
        in_specs.append(pl.BlockSpec((1, bn), lambda j, i: (0, j)))
        args.append(col_scale.reshape(1, n))
    return pl.pallas_call(
        functools.partial(_matmul_kernel, scaled=scaled),
        grid=(n // bn, m // bm),
        in_specs=in_specs,
        out_specs=pl.BlockSpec((bm, bn), lambda j, i: (i, j)),
        out_shape=jax.ShapeDtypeStruct((m, n), out_dtype),
        scratch_shapes=[pltpu.VMEM((k, bn), BF16)],
        compiler_params=_params(("parallel", "arbitrary")),
        name="dense_proj",
    )(*args)


def _sb_kernel(q_ref, k_ref, v_ref, tri_ref, w_ref, o_ref, acc_ref, run_ref, *, t, n_group):
    qi = pl.program_id(2)
    tri = tri_ref[...]
    below_diag = (lax.broadcasted_iota(jnp.int32, (t, t), 1) < lax.broadcasted_iota(jnp.int32, (t, t), 0))
    lanes = [slice(g * HEAD_DIM, (g + 1) * HEAD_DIM) for g in range(n_group)]

    def block(j, on_diagonal):
        start = pl.multiple_of(j * t, t)

        def scores(g):
            return lax.dot_general(q_ref[0, :, lanes[g]], k_ref[0, pl.ds(start, t), lanes[g]], NT_DIMS,
                                   preferred_element_type=F32)

        def suffix_sums(z):
            softplus = jnp.maximum(z, 0.0) + jnp.log2(1.0 + jnp.exp2(-jnp.abs(z)))
            log_beta = z - softplus
            if on_diagonal:
                softplus = jnp.where(below_diag, softplus, 0.0)
            hi = softplus.astype(BF16)
            lo = (softplus - hi.astype(F32)).astype(BF16)
            suffix = jnp.dot(jnp.concatenate([hi, lo], axis=1), tri, preferred_element_type=F32)
            return log_beta, suffix, jnp.sum(softplus, axis=1, keepdims=True)

        def weighted_values(g, log_beta, suffix, total):
            vs = v_ref[0, pl.ds(start, t), lanes[g]]
            if on_diagonal:
                a = jnp.where(below_diag, jnp.exp2(log_beta - suffix), 0.0)
                acc_ref[:, lanes[g]] = jnp.dot(a.astype(BF16), vs, preferred_element_type=F32)
                run_ref[g] = total
            else:
                run = run_ref[g]
                a = jnp.exp2(log_beta - suffix - run)
                acc_ref[:, lanes[g]] += jnp.dot(a.astype(BF16), vs, preferred_element_type=F32)
                run_ref[g] = run + total

        z, mid = {}, {}
        for step in range(n_group + 2):
            if step < n_group:
                z[step] = scores(step)
            if 0 <= step - 1 < n_group:
                mid[step - 1] = suffix_sums(z.pop(step - 1))
            if 0 <= step - 2 < n_group:
                weighted_values(step - 2, *mid.pop(step - 2))

    def smallest_run():
        m = run_ref[0]
        for g in range(1, n_group):
            m = jnp.minimum(m, run_ref[g])
        return jnp.min(m)

    block(qi, True)

    def more_to_do(state):
        jj, low = state
        return jnp.logical_and(jj <= qi, low < F32_EXP2_UNDERFLOW)

    def off_diagonal(state):
        jj, _ = state
        block(qi - jj, False)
        return jj + 1, smallest_run()

    lax.while_loop(more_to_do, off_diagonal, (jnp.int32(1), smallest_run()))
    for g in range(n_group):
        acc = acc_ref[:, lanes[g]]
        ms = jnp.mean(acc * acc, axis=-1, keepdims=True)
        o_ref[0, :, lanes[g]] = (acc * lax.rsqrt(ms + NORM_EPS) * w_ref[:, lanes[g]]).astype(o_ref.dtype)


def _sb_attention(proj, norm_w, n_heads, col0, n_group=8, t=256):
    b, s, _ = proj.shape
    width = n_group * HEAD_DIM
    strict_lower = (jnp.arange(t)[:, None] > jnp.arange(t)[None, :]).astype(BF16)
    tri = jnp.concatenate([strict_lower, strict_lower], axis=0)
    kern = functools.partial(_sb_kernel, t=t, n_group=n_group)
    group_col = lambda part: (col0 + part * n_heads) // n_group
    return pl.pallas_call(
        kern,
        grid=(b, n_heads // n_group, s // t),
        in_specs=[pl.BlockSpec((1, t, width), lambda bi, hg, i: (bi, i, group_col(0) + hg)),
                  pl.BlockSpec((1, s, width), lambda bi, hg, i: (bi, 0, group_col(1) + hg)),
                  pl.BlockSpec((1, s, width), lambda bi, hg, i: (bi, 0, group_col(2) + hg)),
                  pl.BlockSpec((2 * t, t), lambda bi, hg, i: (0, 0)),
                  pl.BlockSpec((1, width), lambda bi, hg, i: (0, hg))],
        out_specs=pl.BlockSpec((1, t, width), lambda bi, hg, i: (bi, i, hg)),
        out_shape=jax.ShapeDtypeStruct((b, s, n_heads * HEAD_DIM), BF16),
        scratch_shapes=[pltpu.VMEM((t, width), F32), pltpu.VMEM((n_group, t, 1), F32)],
        compiler_params=_params(("parallel", "parallel", "arbitrary")),
        name="sb_attention",
    )(proj, proj, proj, tri, norm_w.reshape(1, -1).astype(F32))


def _ret_kernel(q_ref, k_ref, v_ref, g_ref, cos_ref, sin_ref, dec_ref, xi_ref, zeta_ref, gt_ref, w_ref,
                o_ref, state_ref, *, n_group):
    @pl.when(pl.program_id(2) == 0)
    def _():
        state_ref[...] = jnp.zeros_like(state_ref)

    cos = cos_ref[...]
    sin = sin_ref[...]
    half = HEAD_DIM // 2
    lanes = [slice(g * HEAD_DIM, (g + 1) * HEAD_DIM) for g in range(n_group)]

    def rotate_and_score(g):
        q = q_ref[0, :, lanes[g]].astype(F32)
        k = k_ref[0, :, lanes[g]].astype(F32)
        qr = q * cos + pltpu.roll(q, half, 1) * sin
        kr = k * cos + pltpu.roll(k, half, 1) * sin
        qb = qr.astype(BF16)
        scores = lax.dot_general(qb, kr.astype(BF16), NT_DIMS, preferred_element_type=F32)
        carried = jnp.dot(qb, state_ref[g].astype(BF16), preferred_element_type=F32)
        return kr, scores, carried

    def mix_values(g, kr, scores, carried):
        v = v_ref[0, :, lanes[g]]
        ret = jnp.dot((scores * dec_ref[g]).astype(BF16), v, preferred_element_type=F32) + carried * xi_ref[g]
        kz = (kr * zeta_ref[g]).astype(BF16)
        state_ref[g] = state_ref[g] * gt_ref[g] + lax.dot_general(kz, v, TN_DIMS, preferred_element_type=F32)
        return ret

    def normalise(g, ret):
        mu = jnp.mean(ret, axis=-1, keepdims=True)
        cen = ret - mu
        var = jnp.mean(cen * cen, axis=-1, keepdims=True)
        gate = g_ref[0, :, lanes[g]].astype(F32)
        silu = gate / (1.0 + jnp.exp(-gate))
        o_ref[0, :, lanes[g]] = (cen * lax.rsqrt(var + NORM_EPS) * w_ref[:, lanes[g]] * silu).astype(o_ref.dtype)

    first, second = {}, {}
    for step in range(n_group + 2):
        if step < n_group:
            first[step] = rotate_and_score(step)
        if 0 <= step - 1 < n_group:
            second[step - 1] = mix_values(step - 1, *first.pop(step - 1))
        if 0 <= step - 2 < n_group:
            normalise(step - 2, second.pop(step - 2))


def _retention(proj, norm_w, n_heads, col0, n_group=4):
    b, s, _ = proj.shape
    t = 256
    d = HEAD_DIM
    inv_freq = ROPE_THETA ** (-jnp.arange(0, d, 2, dtype=F32) / d)
    ang = jnp.arange(s, dtype=F32)[:, None] * inv_freq[None, :]
    ang = jnp.concatenate([ang, ang], -1)
    sign = jnp.concatenate([-jnp.ones((d // 2,), F32), jnp.ones((d // 2,), F32)])
    cos_t = jnp.cos(ang)
    sin_t = jnp.sin(ang) * sign[None, :]
    log_g = jnp.log1p(-(2.0 ** (-5.0 - jnp.arange(n_heads, dtype=F32))))
    idx = jnp.arange(t, dtype=F32)
    same_or_earlier_chunk = (jnp.floor(idx[None, :] / CHUNK) <= jnp.floor(idx[:, None] / CHUNK))
    dec = jnp.exp(log_g[:, None, None] * jnp.abs(idx[:, None] - idx[None, :])) * (d ** -0.5)
    dec = jnp.where(same_or_earlier_chunk[None], dec, 0.0)
    xi = jnp.broadcast_to(jnp.exp(log_g[:, None] * (idx + 1.0))[:, :, None], (n_heads, t, d))
    zeta = jnp.broadcast_to((jnp.exp(log_g[:, None] * (t - 1.0 - idx)) * (d ** -0.5))[:, :, None], (n_heads, t, d))
    gt = jnp.broadcast_to(jnp.exp(log_g * t)[:, None, None], (n_heads, 1, d))
    width = n_group * d
    group_spec = lambda part: pl.BlockSpec(
        (1, t, width), lambda bi, hg, i: (bi, i, (col0 + part * n_heads) // n_group + hg))
    per_head = lambda rows, cols: pl.BlockSpec((n_group, rows, cols), lambda bi, hg, i: (hg, 0, 0))
    return pl.pallas_call(
        functools.partial(_ret_kernel, n_group=n_group),
        grid=(b, n_heads // n_group, s // t),
        in_specs=[group_spec(0), group_spec(1), group_spec(2), group_spec(3),
                  pl.BlockSpec((t, d), lambda bi, hg, i: (i, 0)),
                  pl.BlockSpec((t, d), lambda bi, hg, i: (i, 0)),
                  per_head(t, t), per_head(t, d), per_head(t, d), per_head(1, d),
                  pl.BlockSpec((1, width), lambda bi, hg, i: (0, hg))],
        out_specs=pl.BlockSpec((1, t, width), lambda bi, hg, i: (bi, i, hg)),
        out_shape=jax.ShapeDtypeStruct((b, s, n_heads * d), BF16),
        scratch_shapes=[pltpu.VMEM((n_group, d, d), F32)],
        compiler_params=_params(("parallel", "parallel", "arbitrary")),
        name="retention",
    )(proj, proj, proj, proj, cos_t, sin_t, dec, xi, zeta, gt, norm_w.reshape(1, -1).astype(F32))


def _layer_norm(y, w, b):
    mu = jnp.mean(y, axis=-1, keepdims=True)
    cen = y - mu
    var = jnp.mean(cen * cen, axis=-1, keepdims=True)
    return cen * lax.rsqrt(var + LN_EPS) * w + b


def _outproj_ln_kernel(sb_ref, ret_ref, wo_ref, x_ref, lw_ref, lb_ref, o_ref, ob_ref, *, alpha, split):
    mix = jnp.dot(sb_ref[...], wo_ref[:split, :], preferred_element_type=F32)
    mix = mix + jnp.dot(ret_ref[...], wo_ref[split:, :], preferred_element_type=F32)
    y = _layer_norm(alpha * x_ref[...] + mix, lw_ref[...], lb_ref[...])
    o_ref[...] = y
    ob_ref[...] = y.astype(BF16)


def _outproj_ln(sb, ret, wo, x, lw, lb, alpha):
    m, d = x.shape
    split = sb.shape[1]
    bm = 512
    kern = functools.partial(_outproj_ln_kernel, alpha=alpha, split=split)
    row = lambda width: pl.BlockSpec((bm, width), lambda i: (i, 0))
    full = lambda shape: pl.BlockSpec(shape, lambda i: (0, 0))
    return pl.pallas_call(
        kern,
        grid=(m // bm,),
        in_specs=[row(split), row(ret.shape[1]), full(wo.shape), row(d), full((1, d)), full((1, d))],
        out_specs=[row(d), row(d)],
        out_shape=[jax.ShapeDtypeStruct((m, d), F32), jax.ShapeDtypeStruct((m, d), BF16)],
        compiler_params=_params(("parallel",)),
        name="outproj_ln",
    )(sb, ret, wo, x, lw.reshape(1, d), lb.reshape(1, d))


def _top_ranks(s, iota, k, exact_ties):
    n = s.shape[0]
    work = s
    rank = jnp.full(s.shape, float(k), F32)
    tops = []
    for r in range(k):
        m = jnp.max(work, axis=0, keepdims=True)
        sel = work == m
        if exact_ties:
            first = jnp.min(jnp.where(sel, iota, float(n)), axis=0, keepdims=True)
            sel = iota == first
        rank = jnp.where(sel, float(r), rank)
        work = jnp.where(sel, -jnp.inf, work)
        tops.append(m)
    return rank, tops


def _route_column(s1, s2, iota, ciota, widths, exact_ties):
    k = PEER_TOPK
    nk, lanes = s1.shape
    n_cand = sum(widths)
    n_pad = ciota.shape[0]
    rank1, top1 = _top_ranks(s1, iota, k, exact_ties)
    rank2, top2 = _top_ranks(s2, iota, k, exact_ties)
    top2_all = jnp.concatenate(top2, axis=0)
    cand = [top1[a] + top2_all[:widths[a]] for a in range(k)]
    if n_pad > n_cand:
        cand.append(jnp.full((n_pad - n_cand, lanes), -jnp.inf, F32))
    cand = jnp.concatenate(cand, axis=0)
    crank, _ = _top_ranks(cand, ciota, k, exact_ties)
    chosen = crank < float(k)
    best = top1[0] + top2[0]
    z = jnp.sum(jnp.where(chosen, jnp.exp(cand - best), 0.0), axis=0, keepdims=True)
    ones = jnp.where(chosen, 1.0, 0.0)
    cnt = jnp.zeros((nk, lanes), F32)
    off = 0
    for a in range(k):
        cnt_a = jnp.sum(ones[off:off + widths[a]], axis=0, keepdims=True)
        cnt = jnp.where(rank1 == float(a), cnt_a, cnt)
        off += widths[a]
    picked = (jnp.sum(jnp.where(rank1 < float(k), 1.0, 0.0), axis=0, keepdims=True)
              + jnp.sum(jnp.where(rank2 < float(k), 1.0, 0.0), axis=0, keepdims=True)
              + jnp.sum(ones, axis=0, keepdims=True))
    unique = picked == float(3 * k)
    return (cnt, jnp.exp(s1 - top1[0]), rank2, jnp.exp(s2 - top2[0]) / z), unique


def _router_kernel(q_ref, keys_ref, cnt_ref, e1_ref, rank2_ref, e2_ref, *, lanes, cols_per_iter):
    nk = PEER_KEYS
    k = PEER_TOPK
    iota = lax.broadcasted_iota(jnp.int32, (nk, lanes), 0).astype(F32)
    widths = [k // (a + 1) for a in range(k)]
    n_pad = -(-sum(widths) // F32_SUBLANES) * F32_SUBLANES
    ciota = lax.broadcasted_iota(jnp.int32, (n_pad, lanes), 0).astype(F32)
    out_refs = (cnt_ref, e1_ref, rank2_ref, e2_ref)

    def columns(group, _):
        pending = []
        for j in range(cols_per_iter):
            c = group * cols_per_iter + j
            sl = pl.ds(pl.multiple_of(c * lanes, lanes), lanes)
            q = q_ref[sl, :]
            half = q.shape[1] // 2
            s1 = lax.dot_general(keys_ref[0, 0], q[:, :half], NT_DIMS, preferred_element_type=F32)
            s2 = lax.dot_general(keys_ref[0, 1], q[:, half:], NT_DIMS, preferred_element_type=F32)
            outs, unique = _route_column(s1, s2, iota, ciota, widths, exact_ties=False)
            for ref, val in zip(out_refs, outs):
                ref[0, c] = val
            pending.append((c, s1, s2, jnp.min(jnp.where(unique, 1.0, 0.0))))
        for c, s1, s2, all_unique in pending:
            @pl.when(all_unique < 0.5)
            def _():
                exact, _ = _route_column(s1, s2, iota, ciota, widths, exact_ties=True)
                for ref, val in zip(out_refs, exact):
                    ref[0, c] = val
        return 0

    lax.fori_loop(0, q_ref.shape[0] // (lanes * cols_per_iter), columns, 0)


def _router(qp, sub_keys, tb):
    t = qp.shape[0]
    h, _, nk, half = sub_keys.shape
    out_f32 = jax.ShapeDtypeStruct((h, t // LANES, nk, LANES), F32)
    out_spec = pl.BlockSpec((1, tb // LANES, nk, LANES), lambda j, hh: (hh, j, 0, 0))
    return pl.pallas_call(
        functools.partial(_router_kernel, lanes=LANES, cols_per_iter=4),
        grid=(t // tb, h),
        in_specs=[pl.BlockSpec((tb, 2 * half), lambda j, hh: (j, hh)),
                  pl.BlockSpec((1, 2, nk, half), lambda j, hh: (hh, 0, 0, 0))],
        out_specs=[out_spec, out_spec, out_spec, out_spec],
        out_shape=[out_f32, out_f32, out_f32, out_f32],
        compiler_params=_params(("parallel", "parallel")),
        name="peer_router",
    )(qp, sub_keys)


def _row_to_packed_tile(ref, h, c, row, n_rows):
    rep = jnp.broadcast_to(ref[h, c, 0, row:row + 1, :], (BF16_SUBLANES, LANES))
    packed = rep.astype(BF16)
    return jnp.concatenate([packed] * (n_rows // BF16_SUBLANES), axis=0)


def _peer_kernel(u_ref, v_ref, cnt_ref, e1_ref, rank2_in, e2_in, x_ref, lw_ref, lb_ref, o_ref,
                 act_ref, hid_ref, rank2_ref, e2_ref, xb_ref, *, n_sub, alpha):
    e = pl.program_id(1)
    tb = x_ref.shape[0]
    nk = PEER_KEYS
    n_col = tb // LANES

    @pl.when(e == 0)
    def _():
        xb_ref[...] = x_ref[...].astype(BF16)
        rank2_ref[...] = rank2_in[...].astype(BF16)
        e2_ref[...] = e2_in[...].astype(BF16)
        o_ref[...] = jnp.zeros_like(o_ref)

    act_ref[...] = lax.dot_general(u_ref[...], xb_ref[...], NT_DIMS, preferred_element_type=F32)
    for ii in range(n_sub):
        for c in range(n_col):
            gate = jnp.zeros((nk, LANES), BF16)
            for h in range(PEER_HEADS):
                cnt = _row_to_packed_tile(cnt_ref, h, c, ii, nk)
                e1 = _row_to_packed_tile(e1_ref, h, c, ii, nk)
                gate = gate + jnp.where(rank2_ref[h, c] < cnt, e2_ref[h, c] * e1, jnp.zeros((), BF16))
            a = act_ref[ii * nk:(ii + 1) * nk, c * LANES:(c + 1) * LANES]
            gelu = 0.5 * a * (1.0 + lax.erf(a * (2.0 ** -0.5)))
            hid_ref[c * LANES:(c + 1) * LANES, ii * nk:(ii + 1) * nk] = (gelu.astype(BF16) * gate).T
    o_ref[...] += jnp.dot(hid_ref[...], v_ref[...], preferred_element_type=F32)

    @pl.when(e == pl.num_programs(1) - 1)
    def _():
        o_ref[...] = _layer_norm(alpha * x_ref[...] + o_ref[...], lw_ref[...], lb_ref[...])


def _peer_ffn_ln(x, u, v, cnt, e1, rank2, e2, lw, lb, alpha, tb, eb):
    t, d = x.shape
    n_exp = u.shape[0]
    h, n_col_all, nk, _ = cnt.shape
    assert eb == F32_SUBLANES * nk
    tokens = pl.BlockSpec((tb, d), lambda j, e: (j, 0))
    table = pl.BlockSpec((eb, d), lambda j, e: (e, 0))
    route = pl.BlockSpec((h, tb // LANES, nk, LANES), lambda j, e: (0, j, 0, 0))
    first_key_rows = pl.BlockSpec((h, tb // LANES, 1, F32_SUBLANES, LANES), lambda j, e: (0, j, e, 0, 0))
    vector = pl.BlockSpec((1, d), lambda j, e: (0, 0))
    cnt, e1 = (a.reshape(h, n_col_all, nk // F32_SUBLANES, F32_SUBLANES, LANES) for a in (cnt, e1))
    packed = pltpu.VMEM((h, tb // LANES, nk, LANES), BF16)
    return pl.pallas_call(
        functools.partial(_peer_kernel, n_sub=eb // nk, alpha=alpha),
        grid=(t // tb, n_exp // eb),
        in_specs=[table, table, first_key_rows, first_key_rows, route, route, tokens, vector, vector],
        out_specs=tokens,
        out_shape=jax.ShapeDtypeStruct((t, d), F32),
        scratch_shapes=[pltpu.VMEM((eb, tb), F32), pltpu.VMEM((tb, eb), BF16), packed, packed,
                        pltpu.VMEM((tb, d), BF16)],
        compiler_params=_params(("parallel", "arbitrary")),
        name="peer_ffn_ln",
    )(u, v, cnt, e1, rank2, e2, x, lw.reshape(1, d), lb.reshape(1, d))


def _layer(x, w_in, sb_norm_w, ret_norm_w, w_out, ln1_w, ln1_b, w_pq, sub_keys, expert_u, expert_v,
           ln2_w, ln2_b, alpha):
    b, s, d = x.shape
    n_tok = b * s
    sb_heads = sb_norm_w.shape[0] // HEAD_DIM
    ret_heads = ret_norm_w.shape[0] // HEAD_DIM
    xt = x.reshape(n_tok, d)
    col_scale = jnp.where(jnp.arange(w_in.shape[1]) < sb_heads * HEAD_DIM, SB_Q_SCALE, 1.0).astype(F32)
    proj = _matmul(xt, w_in, 1024, 1024, BF16, col_scale).reshape(b, s, -1)
    sb = _sb_attention(proj, sb_norm_w, sb_heads, 0)
    ret = _retention(proj, ret_norm_w, ret_heads, 3 * sb_heads)
    x1, x1b = _outproj_ln(sb.reshape(n_tok, -1), ret.reshape(n_tok, -1), w_out.astype(BF16), xt,
                          ln1_w, ln1_b, alpha)
    qp = _matmul(x1b, w_pq, 1024, 1024, BF16)
    cnt, e1, rank2, e2 = _router(qp, sub_keys.astype(BF16), 512)
    out = _peer_ffn_ln(x1, expert_u.astype(BF16), expert_v.astype(BF16), cnt, e1, rank2, e2, ln2_w, ln2_b,
                       alpha, 512, 1024)
    return out.reshape(b, s, d)


def kernel(x, w_in, sb_norm_w, ret_norm_w, w_out, ln1_w, ln1_b, w_pq, sub_keys, expert_u, expert_v, ln2_w, ln2_b):
    depth = w_in.shape[0]
    alpha = (2.0 * depth) ** 0.25
    for layer in range(depth):
        x = _layer(x, w_in[layer], sb_norm_w[layer], ret_norm_w[layer], w_out[layer], ln1_w[layer],
                   ln1_b[layer], w_pq[layer], sub_keys[layer], expert_u[layer], expert_v[layer],
                   ln2_w[layer], ln2_b[layer], alpha)
    return x
```

```python
import functools
import math

import jax
import jax.numpy as jnp
from jax import lax
from jax.experimental import pallas as pl
from jax.experimental.pallas import tpu as pltpu

F32 = jnp.float32
BF16 = jnp.bfloat16

LANES = 128
F32_SUBLANES = 8
BF16_SUBLANES = 16
HEAD_DIM = 128
CHUNK = 64
ROPE_THETA = 10000.0
PEER_HEADS = 8
PEER_KEYS = 128
PEER_TOPK = 16
LN_EPS = 1e-5
NORM_EPS = 1e-6
SB_Q_SCALE = HEAD_DIM ** -0.5 * math.log2(math.e)
F32_EXP2_UNDERFLOW = 160.0

V7X_VMEM_BYTES = 64 * 1024 * 1024
VMEM_LIMIT = V7X_VMEM_BYTES - 6 * 1024 * 1024

NT_DIMS = (((1,), (1,)), ((), ()))
TN_DIMS = (((0,), (0,)), ((), ()))


def _params(semantics):
    return pltpu.CompilerParams(dimension_semantics=semantics, vmem_limit_bytes=VMEM_LIMIT)


def _matmul_kernel(x_ref, w_ref, *rest, scaled):
    scale_ref = rest[0] if scaled else None
    o_ref, wb_ref = rest[-2:]

    @pl.when(pl.program_id(1) == 0)
    def _():
        w = w_ref[...] * scale_ref[...] if scaled else w_ref[...]
        wb_ref[...] = w.astype(BF16)

    o_ref[...] = jnp.dot(x_ref[...].astype(BF16), wb_ref[...], preferred_element_type=F32).astype(o_ref.dtype)


def _matmul(x, w, bm, bn, out_dtype, col_scale=None):
    m, k = x.shape
    n = w.shape[1]
    scaled = col_scale is not None
    in_specs = [pl.BlockSpec((bm, k), lambda j, i: (i, 0)), pl.BlockSpec((k, bn), lambda j, i: (0, j))]
    args = [x, w]
    if scaled:
        in_specs.append(pl.BlockSpec((1, bn), lambda j, i: (0, j)))
        args.append(col_scale.reshape(1, n))
    return pl.pallas_call(
        functools.partial(_matmul_kernel, scaled=scaled),
        grid=(n // bn, m // bm),
        in_specs=in_specs,
        out_specs=pl.BlockSpec((bm, bn), lambda j, i: (i, j)),
        out_shape=jax.ShapeDtypeStruct((m, n), out_dtype),
        scratch_shapes=[pltpu.VMEM((k, bn), BF16)],
        compiler_params=_params(("parallel", "arbitrary")),
        name="dense_proj",
    )(*args)


def _sb_kernel(q_ref, k_ref, v_ref, tri_ref, w_ref, o_ref, acc_ref, run_ref, *, t, n_group):
    qi = pl.program_id(2)
    tri = tri_ref[...]
    below_diag = (lax.broadcasted_iota(jnp.int32, (t, t), 1) < lax.broadcasted_iota(jnp.int32, (t, t), 0))
    lanes = [slice(g * HEAD_DIM, (g + 1) * HEAD_DIM) for g in range(n_group)]

    def block(j, on_diagonal):
        start = pl.multiple_of(j * t, t)

        def scores(g):
            return lax.dot_general(q_ref[0, :, lanes[g]], k_ref[0, pl.ds(start, t), lanes[g]], NT_DIMS,
                                   preferred_element_type=F32)

        def suffix_sums(z):
            softplus = jnp.maximum(z, 0.0) + jnp.log2(1.0 + jnp.exp2(-jnp.abs(z)))
            log_beta = z - softplus
            if on_diagonal:
                softplus = jnp.where(below_diag, softplus, 0.0)
            hi = softplus.astype(BF16)
            lo = (softplus - hi.astype(F32)).astype(BF16)
            suffix = jnp.dot(jnp.concatenate([hi, lo], axis=1), tri, preferred_element_type=F32)
            return log_beta, suffix, jnp.sum(softplus, axis=1, keepdims=True)

        def weighted_values(g, log_beta, suffix, total):
            vs = v_ref[0, pl.ds(start, t), lanes[g]]
            if on_diagonal:
                a = jnp.where(below_diag, jnp.exp2(log_beta - suffix), 0.0)
                acc_ref[:, lanes[g]] = jnp.dot(a.astype(BF16), vs, preferred_element_type=F32)
                run_ref[g] = total
            else:
                run = run_ref[g]
                a = jnp.exp2(log_beta - suffix - run)
                acc_ref[:, lanes[g]] += jnp.dot(a.astype(BF16), vs, preferred_element_type=F32)
                run_ref[g] = run + total

        z, mid = {}, {}
        for step in range(n_group + 2):
            if step < n_group:
                z[step] = scores(step)
            if 0 <= step - 1 < n_group:
                mid[step - 1] = suffix_sums(z.pop(step - 1))
            if 0 <= step - 2 < n_group:
                weighted_values(step - 2, *mid.pop(step - 2))

    def smallest_run():
        m = run_ref[0]
        for g in range(1, n_group):
            m = jnp.minimum(m, run_ref[g])
        return jnp.min(m)

    block(qi, True)

    def more_to_do(state):
        jj, low = state
        return jnp.logical_and(jj <= qi, low < F32_EXP2_UNDERFLOW)

    def off_diagonal(state):
        jj, _ = state
        block(qi - jj, False)
        return jj + 1, smallest_run()

    lax.while_loop(more_to_do, off_diagonal, (jnp.int32(1), smallest_run()))
    for g in range(n_group):
        acc = acc_ref[:, lanes[g]]
        ms = jnp.mean(acc * acc, axis=-1, keepdims=True)
        o_ref[0, :, lanes[g]] = (acc * lax.rsqrt(ms + NORM_EPS) * w_ref[:, lanes[g]]).astype(o_ref.dtype)


def _sb_attention(proj, norm_w, n_heads, col0, n_group=8, t=256):
    b, s, _ = proj.shape
    width = n_group * HEAD_DIM
    strict_lower = (jnp.arange(t)[:, None] > jnp.arange(t)[None, :]).astype(BF16)
    tri = jnp.concatenate([strict_lower, strict_lower], axis=0)
    kern = functools.partial(_sb_kernel, t=t, n_group=n_group)
    group_col = lambda part: (col0 + part * n_heads) // n_group
    return pl.pallas_call(
        kern,
        grid=(b, n_heads // n_group, s // t),
        in_specs=[pl.BlockSpec((1, t, width), lambda bi, hg, i: (bi, i, group_col(0) + hg)),
                  pl.BlockSpec((1, s, width), lambda bi, hg, i: (bi, 0, group_col(1) + hg)),
                  pl.BlockSpec((1, s, width), lambda bi, hg, i: (bi, 0, group_col(2) + hg)),
                  pl.BlockSpec((2 * t, t), lambda bi, hg, i: (0, 0)),
                  pl.BlockSpec((1, width), lambda bi, hg, i: (0, hg))],
        out_specs=pl.BlockSpec((1, t, width), lambda bi, hg, i: (bi, i, hg)),
        out_shape=jax.ShapeDtypeStruct((b, s, n_heads * HEAD_DIM), BF16),
        scratch_shapes=[pltpu.VMEM((t, width), F32), pltpu.VMEM((n_group, t, 1), F32)],
        compiler_params=_params(("parallel", "parallel", "arbitrary")),
        name="sb_attention",
    )(proj, proj, proj, tri, norm_w.reshape(1, -1).astype(F32))


def _ret_kernel(q_ref, k_ref, v_ref, g_ref, cos_ref, sin_ref, dec_ref, xi_ref, zeta_ref, gt_ref, w_ref,
                o_ref, state_ref, *, n_group):
    @pl.when(pl.program_id(2) == 0)
    def _():
        state_ref[...] = jnp.zeros_like(state_ref)

    cos = cos_ref[...]
    sin = sin_ref[...]
    half = HEAD_DIM // 2
    lanes = [slice(g * HEAD_DIM, (g + 1) * HEAD_DIM) for g in range(n_group)]

    def rotate_and_score(g):
        q = q_ref[0, :, lanes[g]].astype(F32)
        k = k_ref[0, :, lanes[g]].astype(F32)
        qr = q * cos + pltpu.roll(q, half, 1) * sin
        kr = k * cos + pltpu.roll(k, half, 1) * sin
        qb = qr.astype(BF16)
        scores = lax.dot_general(qb, kr.astype(BF16), NT_DIMS, preferred_element_type=F32)
        carried = jnp.dot(qb, state_ref[g].astype(BF16), preferred_element_type=F32)
        return kr, scores, carried

    def mix_values(g, kr, scores, carried):
        v = v_ref[0, :, lanes[g]]
        ret = jnp.dot((scores * dec_ref[g]).astype(BF16), v, preferred_element_type=F32) + carried * xi_ref[g]
        kz = (kr * zeta_ref[g]).astype(BF16)
        state_ref[g] = state_ref[g] * gt_ref[g] + lax.dot_general(kz, v, TN_DIMS, preferred_element_type=F32)
        return ret

    def normalise(g, ret):
        mu = jnp.mean(ret, axis=-1, keepdims=True)
        cen = ret - mu
        var = jnp.mean(cen * cen, axis=-1, keepdims=True)
        gate = g_ref[0, :, lanes[g]].astype(F32)
        silu = gate / (1.0 + jnp.exp(-gate))
        o_ref[0, :, lanes[g]] = (cen * lax.rsqrt(var + NORM_EPS) * w_ref[:, lanes[g]] * silu).astype(o_ref.dtype)

    first, second = {}, {}
    for step in range(n_group + 2):
        if step < n_group:
            first[step] = rotate_and_score(step)
        if 0 <= step - 1 < n_group:
            second[step - 1] = mix_values(step - 1, *first.pop(step - 1))
        if 0 <= step - 2 < n_group:
            normalise(step - 2, second.pop(step - 2))


def _retention(proj, norm_w, n_heads, col0, n_group=4):
    b, s, _ = proj.shape
    t = 256
    d = HEAD_DIM
    inv_freq = ROPE_THETA ** (-jnp.arange(0, d, 2, dtype=F32) / d)
    ang = jnp.arange(s, dtype=F32)[:, None] * inv_freq[None, :]
    ang = jnp.concatenate([ang, ang], -1)
    sign = jnp.concatenate([-jnp.ones((d // 2,), F32), jnp.ones((d // 2,), F32)])
    cos_t = jnp.cos(ang)
    sin_t = jnp.sin(ang) * sign[None, :]
    log_g = jnp.log1p(-(2.0 ** (-5.0 - jnp.arange(n_heads, dtype=F32))))
    idx = jnp.arange(t, dtype=F32)
    same_or_earlier_chunk = (jnp.floor(idx[None, :] / CHUNK) <= jnp.floor(idx[:, None] / CHUNK))
    dec = jnp.exp(log_g[:, None, None] * jnp.abs(idx[:, None] - idx[None, :])) * (d ** -0.5)
    dec = jnp.where(same_or_earlier_chunk[None], dec, 0.0)
    xi = jnp.broadcast_to(jnp.exp(log_g[:, None] * (idx + 1.0))[:, :, None], (n_heads, t, d))
    zeta = jnp.broadcast_to((jnp.exp(log_g[:, None] * (t - 1.0 - idx)) * (d ** -0.5))[:, :, None], (n_heads, t, d))
    gt = jnp.broadcast_to(jnp.exp(log_g * t)[:, None, None], (n_heads, 1, d))
    width = n_group * d
    group_spec = lambda part: pl.BlockSpec(
        (1, t, width), lambda bi, hg, i: (bi, i, (col0 + part * n_heads) // n_group + hg))
    per_head = lambda rows, cols: pl.BlockSpec((n_group, rows, cols), lambda bi, hg, i: (hg, 0, 0))
    return pl.pallas_call(
        functools.partial(_ret_kernel, n_group=n_group),
        grid=(b, n_heads // n_group, s // t),
        in_specs=[group_spec(0), group_spec(1), group_spec(2), group_spec(3),
                  pl.BlockSpec((t, d), lambda bi, hg, i: (i, 0)),
                  pl.BlockSpec((t, d), lambda bi, hg, i: (i, 0)),
                  per_head(t, t), per_head(t, d), per_head(t, d), per_head(1, d),
                  pl.BlockSpec((1, width), lambda bi, hg, i: (0, hg))],
        out_specs=pl.BlockSpec((1, t, width), lambda bi, hg, i: (bi, i, hg)),
        out_shape=jax.ShapeDtypeStruct((b, s, n_heads * d), BF16),
        scratch_shapes=[pltpu.VMEM((n_group, d, d), F32)],
        compiler_params=_params(("parallel", "parallel", "arbitrary")),
        name="retention",
    )(proj, proj, proj, proj, cos_t, sin_t, dec, xi, zeta, gt, norm_w.reshape(1, -1).astype(F32))


def _layer_norm(y, w, b):
    mu = jnp.mean(y, axis=-1, keepdims=True)
    cen = y - mu
    var = jnp.mean(cen * cen, axis=-1, keepdims=True)
    return cen * lax.rsqrt(var + LN_EPS) * w + b


def _outproj_ln_kernel(sb_ref, ret_ref, wo_ref, x_ref, lw_ref, lb_ref, o_ref, ob_ref, *, alpha, split):
    mix = jnp.dot(sb_ref[...], wo_ref[:split, :], preferred_element_type=F32)
    mix = mix + jnp.dot(ret_ref[...], wo_ref[split:, :], preferred_element_type=F32)
    y = _layer_norm(alpha * x_ref[...] + mix, lw_ref[...], lb_ref[...])
    o_ref[...] = y
    ob_ref[...] = y.astype(BF16)


def _outproj_ln(sb, ret, wo, x, lw, lb, alpha):
    m, d = x.shape
    split = sb.shape[1]
    bm = 512
    kern = functools.partial(_outproj_ln_kernel, alpha=alpha, split=split)
    row = lambda width: pl.BlockSpec((bm, width), lambda i: (i, 0))
    full = lambda shape: pl.BlockSpec(shape, lambda i: (0, 0))
    return pl.pallas_call(
        kern,
        grid=(m // bm,),
        in_specs=[row(split), row(ret.shape[1]), full(wo.shape), row(d), full((1, d)), full((1, d))],
        out_specs=[row(d), row(d)],
        out_shape=[jax.ShapeDtypeStruct((m, d), F32), jax.ShapeDtypeStruct((m, d), BF16)],
        compiler_params=_params(("parallel",)),
        name="outproj_ln",
    )(sb, ret, wo, x, lw.reshape(1, d), lb.reshape(1, d))


def _top_ranks(s, iota, k, exact_ties):
    n = s.shape[0]
    work = s
    rank = jnp.full(s.shape, float(k), F32)
    tops = []
    for r in range(k):
        m = jnp.max(work, axis=0, keepdims=True)
        sel = work == m
        if exact_ties:
            first = jnp.min(jnp.where(sel, iota, float(n)), axis=0, keepdims=True)
            sel = iota == first
        rank = jnp.where(sel, float(r), rank)
        work = jnp.where(sel, -jnp.inf, work)
        tops.append(m)
    return rank, tops


def _route_column(s1, s2, iota, ciota, widths, exact_ties):
    k = PEER_TOPK
    nk, lanes = s1.shape
    n_cand = sum(widths)
    n_pad = ciota.shape[0]
    rank1, top1 = _top_ranks(s1, iota, k, exact_ties)
    rank2, top2 = _top_ranks(s2, iota, k, exact_ties)
    top2_all = jnp.concatenate(top2, axis=0)
    cand = [top1[a] + top2_all[:widths[a]] for a in range(k)]
    if n_pad > n_cand:
        cand.append(jnp.full((n_pad - n_cand, lanes), -jnp.inf, F32))
    cand = jnp.concatenate(cand, axis=0)
    crank, _ = _top_ranks(cand, ciota, k, exact_ties)
    chosen = crank < float(k)
    best = top1[0] + top2[0]
    z = jnp.sum(jnp.where(chosen, jnp.exp(cand - best), 0.0), axis=0, keepdims=True)
    ones = jnp.where(chosen, 1.0, 0.0)
    cnt = jnp.zeros((nk, lanes), F32)
    off = 0
    for a in range(k):
        cnt_a = jnp.sum(ones[off:off + widths[a]], axis=0, keepdims=True)
        cnt = jnp.where(rank1 == float(a), cnt_a, cnt)
        off += widths[a]
    picked = (jnp.sum(jnp.where(rank1 < float(k), 1.0, 0.0), axis=0, keepdims=True)
              + jnp.sum(jnp.where(rank2 < float(k), 1.0, 0.0), axis=0, keepdims=True)
              + jnp.sum(ones, axis=0, keepdims=True))
    unique = picked == float(3 * k)
    return (cnt, jnp.exp(s1 - top1[0]), rank2, jnp.exp(s2 - top2[0]) / z), unique


def _router_kernel(q_ref, keys_ref, first_ref, second_ref, *, lanes, cols_per_iter):
    nk = PEER_KEYS
    k = PEER_TOPK
    iota = lax.broadcasted_iota(jnp.int32, (nk, lanes), 0).astype(F32)
    widths = [k // (a + 1) for a in range(k)]
    n_pad = -(-sum(widths) // F32_SUBLANES) * F32_SUBLANES
    ciota = lax.broadcasted_iota(jnp.int32, (n_pad, lanes), 0).astype(F32)
    out_slots = ((first_ref, 0), (first_ref, 1), (second_ref, 0), (second_ref, 1))

    def columns(group, _):
        pending = []
        for j in range(cols_per_iter):
            c = group * cols_per_iter + j
            sl = pl.ds(pl.multiple_of(c * lanes, lanes), lanes)
            q = q_ref[sl, :]
            half = q.shape[1] // 2
            s1 = lax.dot_general(keys_ref[0, 0], q[:, :half], NT_DIMS, preferred_element_type=F32)
            s2 = lax.dot_general(keys_ref[0, 1], q[:, half:], NT_DIMS, preferred_element_type=F32)
            outs, unique = _route_column(s1, s2, iota, ciota, widths, exact_ties=False)
            for (ref, slot), val in zip(out_slots, outs):
                ref[slot, 0, c] = val
            pending.append((c, s1, s2, jnp.min(jnp.where(unique, 1.0, 0.0))))
        for c, s1, s2, all_unique in pending:
            @pl.when(all_unique < 0.5)
            def _():
                exact, _ = _route_column(s1, s2, iota, ciota, widths, exact_ties=True)
                for (ref, slot), val in zip(out_slots, exact):
                    ref[slot, 0, c] = val
        return 0

    lax.fori_loop(0, q_ref.shape[0] // (lanes * cols_per_iter), columns, 0)


def _router(qp, sub_keys, tb):
    t = qp.shape[0]
    h, _, nk, half = sub_keys.shape
    out_f32 = jax.ShapeDtypeStruct((2, h, t // LANES, nk, LANES), F32)
    out_spec = pl.BlockSpec((2, 1, tb // LANES, nk, LANES), lambda j, hh: (0, hh, j, 0, 0))
    return pl.pallas_call(
        functools.partial(_router_kernel, lanes=LANES, cols_per_iter=4),
        grid=(t // tb, h),
        in_specs=[pl.BlockSpec((tb, 2 * half), lambda j, hh: (j, hh)),
                  pl.BlockSpec((1, 2, nk, half), lambda j, hh: (hh, 0, 0, 0))],
        out_specs=[out_spec, out_spec],
        out_shape=[out_f32, out_f32],
        compiler_params=_params(("parallel", "parallel")),
        name="peer_router",
    )(qp, sub_keys)


def _row_to_packed_tile(ref, k, h, c, row, n_rows):
    rep = jnp.broadcast_to(ref[k, h, c, 0, row:row + 1, :], (BF16_SUBLANES, LANES))
    packed = rep.astype(BF16)
    return jnp.concatenate([packed] * (n_rows // BF16_SUBLANES), axis=0)


def _peer_kernel(u_ref, v_ref, first_ref, second_in, x_ref, ln_ref, o_ref,
                 act_ref, hid_ref, rank2_ref, e2_ref, xb_ref, *, n_sub, alpha):
    e = pl.program_id(1)
    tb = x_ref.shape[0]
    nk = PEER_KEYS
    n_col = tb // LANES

    @pl.when(e == 0)
    def _():
        xb_ref[...] = x_ref[...].astype(BF16)
        rank2_ref[...] = second_in[0].astype(BF16)
        e2_ref[...] = second_in[1].astype(BF16)
        o_ref[...] = jnp.zeros_like(o_ref)

    act_ref[...] = lax.dot_general(u_ref[...], xb_ref[...], NT_DIMS, preferred_element_type=F32)
    for ii in range(n_sub):
        for c in range(n_col):
            gate = jnp.zeros((nk, LANES), BF16)
            for h in range(PEER_HEADS):
                cnt = _row_to_packed_tile(first_ref, 0, h, c, ii, nk)
                e1 = _row_to_packed_tile(first_ref, 1, h, c, ii, nk)
                gate = gate + jnp.where(rank2_ref[h, c] < cnt, e2_ref[h, c] * e1, jnp.zeros((), BF16))
            a = act_ref[ii * nk:(ii + 1) * nk, c * LANES:(c + 1) * LANES]
            gelu = 0.5 * a * (1.0 + lax.erf(a * (2.0 ** -0.5)))
            hid_ref[c * LANES:(c + 1) * LANES, ii * nk:(ii + 1) * nk] = (gelu.astype(BF16) * gate).T
    o_ref[...] += jnp.dot(hid_ref[...], v_ref[...], preferred_element_type=F32)

    @pl.when(e == pl.num_programs(1) - 1)
    def _():
        o_ref[...] = _layer_norm(alpha * x_ref[...] + o_ref[...], ln_ref[0:1, :], ln_ref[1:2, :])


def _peer_ffn_ln(x, u, v, first, second, lw, lb, alpha, tb, eb):
    t, d = x.shape
    n_exp = u.shape[0]
    _, h, n_col_all, nk, _ = first.shape
    assert eb == F32_SUBLANES * nk
    tokens = pl.BlockSpec((tb, d), lambda j, e: (j, 0))
    table = pl.BlockSpec((eb, d), lambda j, e: (e, 0))
    route = pl.BlockSpec((2, h, tb // LANES, nk, LANES), lambda j, e: (0, 0, j, 0, 0))
    first_key_rows = pl.BlockSpec((2, h, tb // LANES, 1, F32_SUBLANES, LANES), lambda j, e: (0, 0, j, e, 0, 0))
    ln_params = pl.BlockSpec((2, d), lambda j, e: (0, 0))
    first = first.reshape(2, h, n_col_all, nk // F32_SUBLANES, F32_SUBLANES, LANES)
    packed = pltpu.VMEM((h, tb // LANES, nk, LANES), BF16)
    return pl.pallas_call(
        functools.partial(_peer_kernel, n_sub=eb // nk, alpha=alpha),
        grid=(t // tb, n_exp // eb),
        in_specs=[table, table, first_key_rows, route, tokens, ln_params],
        out_specs=tokens,
        out_shape=jax.ShapeDtypeStruct((t, d), F32),
        scratch_shapes=[pltpu.VMEM((eb, tb), F32), pltpu.VMEM((tb, eb), BF16), packed, packed,
                        pltpu.VMEM((tb, d), BF16)],
        compiler_params=_params(("parallel", "arbitrary")),
        name="peer_ffn_ln",
    )(u, v, first, second, x, jnp.stack([lw, lb]))


def _layer(x, w_in, sb_norm_w, ret_norm_w, w_out, ln1_w, ln1_b, w_pq, sub_keys, expert_u, expert_v,
           ln2_w, ln2_b, alpha):
    b, s, d = x.shape
    n_tok = b * s
    sb_heads = sb_norm_w.shape[0] // HEAD_DIM
    ret_heads = ret_norm_w.shape[0] // HEAD_DIM
    xt = x.reshape(n_tok, d)
    col_scale = jnp.where(jnp.arange(w_in.shape[1]) < sb_heads * HEAD_DIM, SB_Q_SCALE, 1.0).astype(F32)
    proj = _matmul(xt, w_in, 1024, 1024, BF16, col_scale).reshape(b, s, -1)
    sb = _sb_attention(proj, sb_norm_w, sb_heads, 0)
    ret = _retention(proj, ret_norm_w, ret_heads, 3 * sb_heads)
    x1, x1b = _outproj_ln(sb.reshape(n_tok, -1), ret.reshape(n_tok, -1), w_out.astype(BF16), xt,
                          ln1_w, ln1_b, alpha)
    qp = _matmul(x1b, w_pq, 1024, 1024, BF16)
    first, second = _router(qp, sub_keys.astype(BF16), 512)
    out = _peer_ffn_ln(x1, expert_u.astype(BF16), expert_v.astype(BF16), first, second, ln2_w, ln2_b,
                       alpha, 512, 1024)
    return out.reshape(b, s, d)


def kernel(x, w_in, sb_norm_w, ret_norm_w, w_out, ln1_w, ln1_b, w_pq, sub_keys, expert_u, expert_v, ln2_w, ln2_b):
    depth = w_in.shape[0]
    alpha = (2.0 * depth) ** 0.25
    for layer in range(depth):
        x = _layer(x, w_in[layer], sb_norm_w[layer], ret_norm_w[layer], w_out[layer], ln1_w[layer],
                   ln1_b[layer], w_pq[layer], sub_keys[layer], expert_u[layer], expert_v[layer],
                   ln2_w[layer], ln2_b[layer], alpha)
    return x
```

```python
import functools
import math

import jax
import jax.numpy as jnp
from jax import lax
from jax.experimental import pallas as pl
from jax.experimental.pallas import tpu as pltpu

F32 = jnp.float32
BF16 = jnp.bfloat16

LANES = 128
F32_SUBLANES = 8
BF16_SUBLANES = 16
HEAD_DIM = 128
CHUNK = 64
ROPE_THETA = 10000.0
PEER_HEADS = 8
PEER_KEYS = 128
PEER_TOPK = 16
LN_EPS = 1e-5
NORM_EPS = 1e-6
SB_Q_SCALE = HEAD_DIM ** -0.5 * math.log2(math.e)
F32_EXP2_UNDERFLOW = 160.0

V7X_VMEM_BYTES = 64 * 1024 * 1024
VMEM_LIMIT = V7X_VMEM_BYTES - 6 * 1024 * 1024

NT_DIMS = (((1,), (1,)), ((), ()))
TN_DIMS = (((0,), (0,)), ((), ()))


def _params(semantics):
    return pltpu.CompilerParams(dimension_semantics=semantics, vmem_limit_bytes=VMEM_LIMIT)


def _matmul_kernel(x_ref, w_ref, *rest, scaled):
    scale_ref = rest[0] if scaled else None
    o_ref, wb_ref = rest[-2:]

    @pl.when(pl.program_id(1) == 0)
    def _():
        w = w_ref[...] * scale_ref[...] if scaled else w_ref[...]
        wb_ref[...] = w.astype(BF16)

    o_ref[...] = jnp.dot(x_ref[...].astype(BF16), wb_ref[...], preferred_element_type=F32).astype(o_ref.dtype)


def _matmul(x, w, bm, bn, out_dtype, col_scale=None):
    m, k = x.shape
    n = w.shape[1]
    scaled = col_scale is not None
    in_specs = [pl.BlockSpec((bm, k), lambda j, i: (i, 0)), pl.BlockSpec((k, bn), lambda j, i: (0, j))]
    args = [x, w]
    if scaled:
        in_specs.append(pl.BlockSpec((1, bn), lambda j, i: (0, j)))
        args.append(col_scale.reshape(1, n))
    return pl.pallas_call(
        functools.partial(_matmul_kernel, scaled=scaled),
        grid=(n // bn, m // bm),
        in_specs=in_specs,
        out_specs=pl.BlockSpec((bm, bn), lambda j, i: (i, j)),
        out_shape=jax.ShapeDtypeStruct((m, n), out_dtype),
        scratch_shapes=[pltpu.VMEM((k, bn), BF16)],
        compiler_params=_params(("parallel", "arbitrary")),
        name="dense_proj",
    )(*args)


def _sb_kernel(q_ref, k_ref, v_ref, tri_ref, w_ref, o_ref, acc_ref, run_ref, *, t, n_group):
    qi = pl.program_id(2)
    tri = tri_ref[...]
    below_diag = (lax.broadcasted_iota(jnp.int32, (t, t), 1) < lax.broadcasted_iota(jnp.int32, (t, t), 0))
    lanes = [slice(g * HEAD_DIM, (g + 1) * HEAD_DIM) for g in range(n_group)]

    def block(j, on_diagonal):
        start = pl.multiple_of(j * t, t)

        def scores(g):
            return lax.dot_general(q_ref[0, :, lanes[g]], k_ref[0, pl.ds(start, t), lanes[g]], NT_DIMS,
                                   preferred_element_type=F32)

        def suffix_sums(z):
            softplus = jnp.maximum(z, 0.0) + jnp.log2(1.0 + jnp.exp2(-jnp.abs(z)))
            log_beta = z - softplus
            if on_diagonal:
                softplus = jnp.where(below_diag, softplus, 0.0)
            hi = softplus.astype(BF16)
            lo = (softplus - hi.astype(F32)).astype(BF16)
            suffix = jnp.dot(jnp.concatenate([hi, lo], axis=1), tri, preferred_element_type=F32)
            return log_beta, suffix, jnp.sum(softplus, axis=1, keepdims=True)

        def weighted_values(g, log_beta, suffix, total):
            vs = v_ref[0, pl.ds(start, t), lanes[g]]
            if on_diagonal:
                a = jnp.where(below_diag, jnp.exp2(log_beta - suffix), 0.0)
                acc_ref[:, lanes[g]] = jnp.dot(a.astype(BF16), vs, preferred_element_type=F32)
                run_ref[g] = total
            else:
                run = run_ref[g]
                a = jnp.exp2(log_beta - suffix - run)
                acc_ref[:, lanes[g]] += jnp.dot(a.astype(BF16), vs, preferred_element_type=F32)
                run_ref[g] = run + total

        z, mid = {}, {}
        for step in range(n_group + 2):
            if step < n_group:
                z[step] = scores(step)
            if 0 <= step - 1 < n_group:
                mid[step - 1] = suffix_sums(z.pop(step - 1))
            if 0 <= step - 2 < n_group:
                weighted_values(step - 2, *mid.pop(step - 2))

    def smallest_run():
        m = run_ref[0]
        for g in range(1, n_group):
            m = jnp.minimum(m, run_ref[g])
        return jnp.min(m)

    block(qi, True)

    def more_to_do(state):
        jj, low = state
        return jnp.logical_and(jj <= qi, low < F32_EXP2_UNDERFLOW)

    def off_diagonal(state):
        jj, _ = state
        block(qi - jj, False)
        return jj + 1, smallest_run()

    lax.while_loop(more_to_do, off_diagonal, (jnp.int32(1), smallest_run()))
    for g in range(n_group):
        acc = acc_ref[:, lanes[g]]
        ms = jnp.mean(acc * acc, axis=-1, keepdims=True)
        o_ref[0, :, lanes[g]] = (acc * lax.rsqrt(ms + NORM_EPS) * w_ref[:, lanes[g]]).astype(o_ref.dtype)


def _sb_attention(proj, norm_w, n_heads, col0, n_group=8, t=256):
    b, s, _ = proj.shape
    width = n_group * HEAD_DIM
    strict_lower = (jnp.arange(t)[:, None] > jnp.arange(t)[None, :]).astype(BF16)
    tri = jnp.concatenate([strict_lower, strict_lower], axis=0)
    kern = functools.partial(_sb_kernel, t=t, n_group=n_group)
    group_col = lambda part: (col0 + part * n_heads) // n_group
    return pl.pallas_call(
        kern,
        grid=(b, n_heads // n_group, s // t),
        in_specs=[pl.BlockSpec((1, t, width), lambda bi, hg, i: (bi, i, group_col(0) + hg)),
                  pl.BlockSpec((1, s, width), lambda bi, hg, i: (bi, 0, group_col(1) + hg)),
                  pl.BlockSpec((1, s, width), lambda bi, hg, i: (bi, 0, group_col(2) + hg)),
                  pl.BlockSpec((2 * t, t), lambda bi, hg, i: (0, 0)),
                  pl.BlockSpec((1, width), lambda bi, hg, i: (0, hg))],
        out_specs=pl.BlockSpec((1, t, width), lambda bi, hg, i: (bi, i, hg)),
        out_shape=jax.ShapeDtypeStruct((b, s, n_heads * HEAD_DIM), BF16),
        scratch_shapes=[pltpu.VMEM((t, width), F32), pltpu.VMEM((n_group, t, 1), F32)],
        compiler_params=_params(("parallel", "parallel", "arbitrary")),
        name="sb_attention",
    )(proj, proj, proj, tri, norm_w.reshape(1, -1).astype(F32))


def _ret_kernel(q_ref, k_ref, v_ref, g_ref, cos_ref, sin_ref, dec_ref, xi_ref, zeta_ref, gt_ref, w_ref,
                o_ref, state_ref, *, n_group):
    @pl.when(pl.program_id(2) == 0)
    def _():
        state_ref[...] = jnp.zeros_like(state_ref)

    cos = cos_ref[...]
    sin = sin_ref[...]
    half = HEAD_DIM // 2
    lanes = [slice(g * HEAD_DIM, (g + 1) * HEAD_DIM) for g in range(n_group)]

    def rotate_and_score(g):
        q = q_ref[0, :, lanes[g]].astype(F32)
        k = k_ref[0, :, lanes[g]].astype(F32)
        qr = q * cos + pltpu.roll(q, half, 1) * sin
        kr = k * cos + pltpu.roll(k, half, 1) * sin
        qb = qr.astype(BF16)
        scores = lax.dot_general(qb, kr.astype(BF16), NT_DIMS, preferred_element_type=F32)
        carried = jnp.dot(qb, state_ref[g].astype(BF16), preferred_element_type=F32)
        return kr, scores, carried

    def mix_values(g, kr, scores, carried):
        v = v_ref[0, :, lanes[g]]
        ret = jnp.dot((scores * dec_ref[g]).astype(BF16), v, preferred_element_type=F32) + carried * xi_ref[g]
        kz = (kr * zeta_ref[g]).astype(BF16)
        state_ref[g] = state_ref[g] * gt_ref[g] + lax.dot_general(kz, v, TN_DIMS, preferred_element_type=F32)
        return ret

    def normalise(g, ret):
        mu = jnp.mean(ret, axis=-1, keepdims=True)
        cen = ret - mu
        var = jnp.mean(cen * cen, axis=-1, keepdims=True)
        gate = g_ref[0, :, lanes[g]].astype(F32)
        silu = gate / (1.0 + jnp.exp(-gate))
        o_ref[0, :, lanes[g]] = (cen * lax.rsqrt(var + NORM_EPS) * w_ref[:, lanes[g]] * silu).astype(o_ref.dtype)

    first, second = {}, {}
    for step in range(n_group + 2):
        if step < n_group:
            first[step] = rotate_and_score(step)
        if 0 <= step - 1 < n_group:
            second[step - 1] = mix_values(step - 1, *first.pop(step - 1))
        if 0 <= step - 2 < n_group:
            normalise(step - 2, second.pop(step - 2))


def _retention(proj, norm_w, n_heads, col0, n_group=4):
    b, s, _ = proj.shape
    t = 256
    d = HEAD_DIM
    inv_freq = ROPE_THETA ** (-jnp.arange(0, d, 2, dtype=F32) / d)
    ang = jnp.arange(s, dtype=F32)[:, None] * inv_freq[None, :]
    ang = jnp.concatenate([ang, ang], -1)
    sign = jnp.concatenate([-jnp.ones((d // 2,), F32), jnp.ones((d // 2,), F32)])
    cos_t = jnp.cos(ang)
    sin_t = jnp.sin(ang) * sign[None, :]
    log_g = jnp.log1p(-(2.0 ** (-5.0 - jnp.arange(n_heads, dtype=F32))))
    idx = jnp.arange(t, dtype=F32)
    same_or_earlier_chunk = (jnp.floor(idx[None, :] / CHUNK) <= jnp.floor(idx[:, None] / CHUNK))
    dec = jnp.exp(log_g[:, None, None] * jnp.abs(idx[:, None] - idx[None, :])) * (d ** -0.5)
    dec = jnp.where(same_or_earlier_chunk[None], dec, 0.0)
    xi = jnp.broadcast_to(jnp.exp(log_g[:, None] * (idx + 1.0))[:, :, None], (n_heads, t, d))
    zeta = jnp.broadcast_to((jnp.exp(log_g[:, None] * (t - 1.0 - idx)) * (d ** -0.5))[:, :, None], (n_heads, t, d))
    gt = jnp.broadcast_to(jnp.exp(log_g * t)[:, None, None], (n_heads, 1, d))
    width = n_group * d
    group_spec = lambda part: pl.BlockSpec(
        (1, t, width), lambda bi, hg, i: (bi, i, (col0 + part * n_heads) // n_group + hg))
    per_head = lambda rows, cols: pl.BlockSpec((n_group, rows, cols), lambda bi, hg, i: (hg, 0, 0))
    return pl.pallas_call(
        functools.partial(_ret_kernel, n_group=n_group),
        grid=(b, n_heads // n_group, s // t),
        in_specs=[group_spec(0), group_spec(1), group_spec(2), group_spec(3),
                  pl.BlockSpec((t, d), lambda bi, hg, i: (i, 0)),
                  pl.BlockSpec((t, d), lambda bi, hg, i: (i, 0)),
                  per_head(t, t), per_head(t, d), per_head(t, d), per_head(1, d),
                  pl.BlockSpec((1, width), lambda bi, hg, i: (0, hg))],
        out_specs=pl.BlockSpec((1, t, width), lambda bi, hg, i: (bi, i, hg)),
        out_shape=jax.ShapeDtypeStruct((b, s, n_heads * d), BF16),
        scratch_shapes=[pltpu.VMEM((n_group, d, d), F32)],
        compiler_params=_params(("parallel", "parallel", "arbitrary")),
        name="retention",
    )(proj, proj, proj, proj, cos_t, sin_t, dec, xi, zeta, gt, norm_w.reshape(1, -1).astype(F32))


def _layer_norm(y, w, b):
    mu = jnp.mean(y, axis=-1, keepdims=True)
    cen = y - mu
    var = jnp.mean(cen * cen, axis=-1, keepdims=True)
    return cen * lax.rsqrt(var + LN_EPS) * w + b


def _outproj_ln_kernel(sb_ref, ret_ref, wo_ref, x_ref, lw_ref, lb_ref, o_ref, ob_ref, *, alpha, split):
    mix = jnp.dot(sb_ref[...], wo_ref[:split, :], preferred_element_type=F32)
    mix = mix + jnp.dot(ret_ref[...], wo_ref[split:, :], preferred_element_type=F32)
    y = _layer_norm(alpha * x_ref[...] + mix, lw_ref[...], lb_ref[...])
    o_ref[...] = y
    ob_ref[...] = y.astype(BF16)


def _outproj_ln(sb, ret, wo, x, lw, lb, alpha):
    m, d = x.shape
    split = sb.shape[1]
    bm = 512
    kern = functools.partial(_outproj_ln_kernel, alpha=alpha, split=split)
    row = lambda width: pl.BlockSpec((bm, width), lambda i: (i, 0))
    full = lambda shape: pl.BlockSpec(shape, lambda i: (0, 0))
    return pl.pallas_call(
        kern,
        grid=(m // bm,),
        in_specs=[row(split), row(ret.shape[1]), full(wo.shape), row(d), full((1, d)), full((1, d))],
        out_specs=[row(d), row(d)],
        out_shape=[jax.ShapeDtypeStruct((m, d), F32), jax.ShapeDtypeStruct((m, d), BF16)],
        compiler_params=_params(("parallel",)),
        name="outproj_ln",
    )(sb, ret, wo, x, lw.reshape(1, d), lb.reshape(1, d))


def _top_ranks(s, iota, k, exact_ties):
    n = s.shape[0]
    work = s
    rank = jnp.full(s.shape, float(k), F32)
    tops = []
    for r in range(k):
        m = jnp.max(work, axis=0, keepdims=True)
        sel = work == m
        if exact_ties:
            first = jnp.min(jnp.where(sel, iota, float(n)), axis=0, keepdims=True)
            sel = iota == first
        rank = jnp.where(sel, float(r), rank)
        work = jnp.where(sel, -jnp.inf, work)
        tops.append(m)
    return rank, tops


def _route_column(s1, s2, iota, ciota, widths, exact_ties):
    k = PEER_TOPK
    nk, lanes = s1.shape
    n_cand = sum(widths)
    n_pad = ciota.shape[0]
    rank1, top1 = _top_ranks(s1, iota, k, exact_ties)
    rank2, top2 = _top_ranks(s2, iota, k, exact_ties)
    top2_all = jnp.concatenate(top2, axis=0)
    cand = [top1[a] + top2_all[:widths[a]] for a in range(k)]
    if n_pad > n_cand:
        cand.append(jnp.full((n_pad - n_cand, lanes), -jnp.inf, F32))
    cand = jnp.concatenate(cand, axis=0)
    crank, _ = _top_ranks(cand, ciota, k, exact_ties)
    chosen = crank < float(k)
    best = top1[0] + top2[0]
    z = jnp.sum(jnp.where(chosen, jnp.exp(cand - best), 0.0), axis=0, keepdims=True)
    ones = jnp.where(chosen, 1.0, 0.0)
    cnt = jnp.zeros((nk, lanes), F32)
    off = 0
    for a in range(k):
        cnt_a = jnp.sum(ones[off:off + widths[a]], axis=0, keepdims=True)
        cnt = jnp.where(rank1 == float(a), cnt_a, cnt)
        off += widths[a]
    picked = (jnp.sum(jnp.where(rank1 < float(k), 1.0, 0.0), axis=0, keepdims=True)
              + jnp.sum(jnp.where(rank2 < float(k), 1.0, 0.0), axis=0, keepdims=True)
              + jnp.sum(ones, axis=0, keepdims=True))
    unique = picked == float(3 * k)
    return (cnt, jnp.exp(s1 - top1[0]), rank2, jnp.exp(s2 - top2[0]) / z), unique


def _router_kernel(q_ref, keys_ref, first_ref, second_ref, *, lanes, cols_per_iter):
    nk = PEER_KEYS
    k = PEER_TOPK
    iota = lax.broadcasted_iota(jnp.int32, (nk, lanes), 0).astype(F32)
    widths = [k // (a + 1) for a in range(k)]
    n_pad = -(-sum(widths) // F32_SUBLANES) * F32_SUBLANES
    ciota = lax.broadcasted_iota(jnp.int32, (n_pad, lanes), 0).astype(F32)
    out_slots = ((first_ref, 0), (first_ref, 1), (second_ref, 0), (second_ref, 1))

    def columns(group, _):
        pending = []
        for j in range(cols_per_iter):
            c = group * cols_per_iter + j
            sl = pl.ds(pl.multiple_of(c * lanes, lanes), lanes)
            q = q_ref[sl, :]
            half = q.shape[1] // 2
            s1 = lax.dot_general(keys_ref[0, 0], q[:, :half], NT_DIMS, preferred_element_type=F32)
            s2 = lax.dot_general(keys_ref[0, 1], q[:, half:], NT_DIMS, preferred_element_type=F32)
            outs, unique = _route_column(s1, s2, iota, ciota, widths, exact_ties=False)
            for (ref, slot), val in zip(out_slots, outs):
                ref[slot, 0, c] = val
            pending.append((c, s1, s2, jnp.min(jnp.where(unique, 1.0, 0.0))))
        for c, s1, s2, all_unique in pending:
            @pl.when(all_unique < 0.5)
            def _():
                exact, _ = _route_column(s1, s2, iota, ciota, widths, exact_ties=True)
                for (ref, slot), val in zip(out_slots, exact):
                    ref[slot, 0, c] = val
        return 0

    lax.fori_loop(0, q_ref.shape[0] // (lanes * cols_per_iter), columns, 0)


def _router(qp, sub_keys, tb):
    t = qp.shape[0]
    h, _, nk, half = sub_keys.shape
    out_f32 = jax.ShapeDtypeStruct((2, h, t // LANES, nk, LANES), F32)
    out_spec = pl.BlockSpec((2, 1, tb // LANES, nk, LANES), lambda j, hh: (0, hh, j, 0, 0))
    return pl.pallas_call(
        functools.partial(_router_kernel, lanes=LANES, cols_per_iter=4),
        grid=(t // tb, h),
        in_specs=[pl.BlockSpec((tb, 2 * half), lambda j, hh: (j, hh)),
                  pl.BlockSpec((1, 2, nk, half), lambda j, hh: (hh, 0, 0, 0))],
        out_specs=[out_spec, out_spec],
        out_shape=[out_f32, out_f32],
        compiler_params=_params(("parallel", "parallel")),
        name="peer_router",
    )(qp, sub_keys)


def _row_to_packed_tile(ref, k, h, c, row, n_rows):
    rep = jnp.broadcast_to(ref[k, h, c, 0, row:row + 1, :], (BF16_SUBLANES, LANES))
    packed = rep.astype(BF16)
    return jnp.concatenate([packed] * (n_rows // BF16_SUBLANES), axis=0)


def _peer_kernel(u_ref, v_ref, first_ref, second_in, x_ref, ln_ref, o_ref,
                 act_ref, hid_ref, rank2_ref, e2_ref, xb_ref, *, n_sub, alpha):
    e = pl.program_id(1)
    tb = x_ref.shape[0]
    nk = PEER_KEYS
    n_col = tb // LANES

    @pl.when(e == 0)
    def _():
        xb_ref[...] = x_ref[...].astype(BF16)
        rank2_ref[...] = second_in[0].astype(BF16)
        e2_ref[...] = second_in[1].astype(BF16)
        o_ref[...] = jnp.zeros_like(o_ref)

    act_ref[...] = lax.dot_general(u_ref[...], xb_ref[...], NT_DIMS, preferred_element_type=F32)
    for ii in range(n_sub):
        for c in range(n_col):
            gate = jnp.zeros((nk, LANES), BF16)
            for h in range(PEER_HEADS):
                cnt = _row_to_packed_tile(first_ref, 0, h, c, ii, nk)
                e1 = _row_to_packed_tile(first_ref, 1, h, c, ii, nk)
                gate = gate + jnp.where(rank2_ref[h, c] < cnt, e2_ref[h, c] * e1, jnp.zeros((), BF16))
            a = act_ref[ii * nk:(ii + 1) * nk, c * LANES:(c + 1) * LANES]
            gelu = 0.5 * a * (1.0 + lax.erf(a * (2.0 ** -0.5)))
            hid_ref[c * LANES:(c + 1) * LANES, ii * nk:(ii + 1) * nk] = (gelu.astype(BF16) * gate).T
    o_ref[...] += jnp.dot(hid_ref[...], v_ref[...], preferred_element_type=F32)

    @pl.when(e == pl.num_programs(1) - 1)
    def _():
        o_ref[...] = _layer_norm(alpha * x_ref[...] + o_ref[...], ln_ref[0:1, :], ln_ref[1:2, :])


def _peer_ffn_ln(x, u, v, first, second, lw, lb, alpha, tb, eb):
    t, d = x.shape
    n_exp = u.shape[0]
    _, h, n_col_all, nk, _ = first.shape
    assert eb == F32_SUBLANES * nk
    tokens = pl.BlockSpec((tb, d), lambda j, e: (j, 0))
    table = pl.BlockSpec((eb, d), lambda j, e: (e, 0))
    route = pl.BlockSpec((2, h, tb // LANES, nk, LANES), lambda j, e: (0, 0, j, 0, 0))
    first_key_rows = pl.BlockSpec((2, h, tb // LANES, 1, F32_SUBLANES, LANES), lambda j, e: (0, 0, j, e, 0, 0))
    ln_params = pl.BlockSpec((2, d), lambda j, e: (0, 0))
    first = first.reshape(2, h, n_col_all, nk // F32_SUBLANES, F32_SUBLANES, LANES)
    packed = pltpu.VMEM((h, tb // LANES, nk, LANES), BF16)
    return pl.pallas_call(
        functools.partial(_peer_kernel, n_sub=eb // nk, alpha=alpha),
        grid=(t // tb, n_exp // eb),
        in_specs=[table, table, first_key_rows, route, tokens, ln_params],
        out_specs=tokens,
        out_shape=jax.ShapeDtypeStruct((t, d), F32),
        scratch_shapes=[pltpu.VMEM((eb, tb), F32), pltpu.VMEM((tb, eb), BF16), packed, packed,
                        pltpu.VMEM((tb, d), BF16)],
        compiler_params=_params(("parallel", "arbitrary")),
        name="peer_ffn_ln",
    )(u, v, first, second, x, jnp.stack([lw, lb]))


def _layer(x, w_in, sb_norm_w, ret_norm_w, w_out, ln1_w, ln1_b, w_pq, sub_keys, expert_u, expert_v,
           ln2_w, ln2_b, alpha):
    b, s, d = x.shape
    n_tok = b * s
    sb_heads = sb_norm_w.shape[0] // HEAD_DIM
    ret_heads = ret_norm_w.shape[0] // HEAD_DIM
    xt = x.reshape(n_tok, d)
    col_scale = jnp.where(jnp.arange(w_in.shape[1]) < sb_heads * HEAD_DIM, SB_Q_SCALE, 1.0).astype(F32)
    proj = _matmul(xt, w_in, 1024, 1024, BF16, col_scale).reshape(b, s, -1)
    sb = _sb_attention(proj, sb_norm_w, sb_heads, 0)
    ret = _retention(proj, ret_norm_w, ret_heads, 3 * sb_heads)
    x1, x1b = _outproj_ln(sb.reshape(n_tok, -1), ret.reshape(n_tok, -1), w_out.astype(BF16), xt,
                          ln1_w, ln1_b, alpha)
    qp = _matmul(x1b, w_pq, 1024, 1024, BF16)
    first, second = _router(qp, sub_keys.astype(BF16), 1024)
    out = _peer_ffn_ln(x1, expert_u.astype(BF16), expert_v.astype(BF16), first, second, ln2_w, ln2_b,
                       alpha, 512, 1024)
    return out.reshape(b, s, d)


def kernel(x, w_in, sb_norm_w, ret_norm_w, w_out, ln1_w, ln1_b, w_pq, sub_keys, expert_u, expert_v, ln2_w, ln2_b):
    depth = w_in.shape[0]
    alpha = (2.0 * depth) ** 0.25
    for layer in range(depth):
        x = _layer(x, w_in[layer], sb_norm_w[layer], ret_norm_w[layer], w_out[layer], ln1_w[layer],
                   ln1_b[layer], w_pq[layer], sub_keys[layer], expert_u[layer], expert_v[layer],
                   ln2_w[layer], ln2_b[layer], alpha)
    return x
```

```python
import functools
import math

import jax
import jax.numpy as jnp
from jax import lax
from jax.experimental import pallas as pl
from jax.experimental.pallas import tpu as pltpu

F32 = jnp.float32
BF16 = jnp.bfloat16

LANES = 128
F32_SUBLANES = 8
BF16_SUBLANES = 16
HEAD_DIM = 128
CHUNK = 64
ROPE_THETA = 10000.0
PEER_HEADS = 8
PEER_KEYS = 128
PEER_TOPK = 16
LN_EPS = 1e-5
NORM_EPS = 1e-6
SB_Q_SCALE = HEAD_DIM ** -0.5 * math.log2(math.e)
F32_EXP2_UNDERFLOW = 160.0

V7X_VMEM_BYTES = 64 * 1024 * 1024
VMEM_LIMIT = V7X_VMEM_BYTES - 6 * 1024 * 1024

NT_DIMS = (((1,), (1,)), ((), ()))
TN_DIMS = (((0,), (0,)), ((), ()))


def _params(semantics):
    return pltpu.CompilerParams(dimension_semantics=semantics, vmem_limit_bytes=VMEM_LIMIT)


def _matmul_kernel(x_ref, w_ref, *rest, scaled):
    scale_ref = rest[0] if scaled else None
    o_ref, wb_ref = rest[-2:]

    @pl.when(pl.program_id(1) == 0)
    def _():
        w = w_ref[...] * scale_ref[...] if scaled else w_ref[...]
        wb_ref[...] = w.astype(BF16)

    o_ref[...] = jnp.dot(x_ref[...].astype(BF16), wb_ref[...], preferred_element_type=F32).astype(o_ref.dtype)


def _matmul(x, w, bm, bn, out_dtype, col_scale=None):
    m, k = x.shape
    n = w.shape[1]
    scaled = col_scale is not None
    in_specs = [pl.BlockSpec((bm, k), lambda j, i: (i, 0)), pl.BlockSpec((k, bn), lambda j, i: (0, j))]
    args = [x, w]
    if scaled:
        in_specs.append(pl.BlockSpec((1, bn), lambda j, i: (0, j)))
        args.append(col_scale.reshape(1, n))
    return pl.pallas_call(
        functools.partial(_matmul_kernel, scaled=scaled),
        grid=(n // bn, m // bm),
        in_specs=in_specs,
        out_specs=pl.BlockSpec((bm, bn), lambda j, i: (i, j)),
        out_shape=jax.ShapeDtypeStruct((m, n), out_dtype),
        scratch_shapes=[pltpu.VMEM((k, bn), BF16)],
        compiler_params=_params(("parallel", "arbitrary")),
        name="dense_proj",
    )(*args)


def _sb_kernel(q_ref, k_ref, v_ref, tri_ref, w_ref, o_ref, acc_ref, run_ref, *, t, n_group):
    qi = pl.program_id(2)
    tri = tri_ref[...]
    below_diag = (lax.broadcasted_iota(jnp.int32, (t, t), 1) < lax.broadcasted_iota(jnp.int32, (t, t), 0))
    lanes = [slice(g * HEAD_DIM, (g + 1) * HEAD_DIM) for g in range(n_group)]

    def block(j, on_diagonal):
        start = pl.multiple_of(j * t, t)

        def scores(g):
            return lax.dot_general(q_ref[0, :, lanes[g]], k_ref[0, pl.ds(start, t), lanes[g]], NT_DIMS,
                                   preferred_element_type=F32)

        def suffix_sums(z):
            softplus = jnp.maximum(z, 0.0) + jnp.log2(1.0 + jnp.exp2(-jnp.abs(z)))
            log_beta = z - softplus
            if on_diagonal:
                softplus = jnp.where(below_diag, softplus, 0.0)
            hi = softplus.astype(BF16)
            lo = (softplus - hi.astype(F32)).astype(BF16)
            suffix = jnp.dot(jnp.concatenate([hi, lo], axis=1), tri, preferred_element_type=F32)
            return log_beta, suffix, jnp.sum(softplus, axis=1, keepdims=True)

        def weighted_values(g, log_beta, suffix, total):
            vs = v_ref[0, pl.ds(start, t), lanes[g]]
            if on_diagonal:
                a = jnp.where(below_diag, jnp.exp2(log_beta - suffix), 0.0)
                acc_ref[:, lanes[g]] = jnp.dot(a.astype(BF16), vs, preferred_element_type=F32)
                run_ref[g] = total
            else:
                run = run_ref[g]
                a = jnp.exp2(log_beta - suffix - run)
                acc_ref[:, lanes[g]] += jnp.dot(a.astype(BF16), vs, preferred_element_type=F32)
                run_ref[g] = run + total

        z, mid = {}, {}
        for step in range(n_group + 2):
            if step < n_group:
                z[step] = scores(step)
            if 0 <= step - 1 < n_group:
                mid[step - 1] = suffix_sums(z.pop(step - 1))
            if 0 <= step - 2 < n_group:
                weighted_values(step - 2, *mid.pop(step - 2))

    def smallest_run():
        m = run_ref[0]
        for g in range(1, n_group):
            m = jnp.minimum(m, run_ref[g])
        return jnp.min(m)

    block(qi, True)

    def more_to_do(state):
        jj, low = state
        return jnp.logical_and(jj <= qi, low < F32_EXP2_UNDERFLOW)

    def off_diagonal(state):
        jj, _ = state
        block(qi - jj, False)
        return jj + 1, smallest_run()

    lax.while_loop(more_to_do, off_diagonal, (jnp.int32(1), smallest_run()))
    for g in range(n_group):
        acc = acc_ref[:, lanes[g]]
        ms = jnp.mean(acc * acc, axis=-1, keepdims=True)
        o_ref[0, :, lanes[g]] = (acc * lax.rsqrt(ms + NORM_EPS) * w_ref[:, lanes[g]]).astype(o_ref.dtype)


def _sb_attention(proj, norm_w, n_heads, col0, n_group=8, t=256):
    b, s, _ = proj.shape
    width = n_group * HEAD_DIM
    strict_lower = (jnp.arange(t)[:, None] > jnp.arange(t)[None, :]).astype(BF16)
    tri = jnp.concatenate([strict_lower, strict_lower], axis=0)
    kern = functools.partial(_sb_kernel, t=t, n_group=n_group)
    group_col = lambda part: (col0 + part * n_heads) // n_group
    return pl.pallas_call(
        kern,
        grid=(b, n_heads // n_group, s // t),
        in_specs=[pl.BlockSpec((1, t, width), lambda bi, hg, i: (bi, i, group_col(0) + hg)),
                  pl.BlockSpec((1, s, width), lambda bi, hg, i: (bi, 0, group_col(1) + hg)),
                  pl.BlockSpec((1, s, width), lambda bi, hg, i: (bi, 0, group_col(2) + hg)),
                  pl.BlockSpec((2 * t, t), lambda bi, hg, i: (0, 0)),
                  pl.BlockSpec((1, width), lambda bi, hg, i: (0, hg))],
        out_specs=pl.BlockSpec((1, t, width), lambda bi, hg, i: (bi, i, hg)),
        out_shape=jax.ShapeDtypeStruct((b, s, n_heads * HEAD_DIM), BF16),
        scratch_shapes=[pltpu.VMEM((t, width), F32), pltpu.VMEM((n_group, t, 1), F32)],
        compiler_params=_params(("parallel", "parallel", "arbitrary")),
        name="sb_attention",
    )(proj, proj, proj, tri, norm_w.reshape(1, -1).astype(F32))


def _ret_kernel(q_ref, k_ref, v_ref, g_ref, cos_ref, sin_ref, dec_ref, xi_ref, zeta_ref, gt_ref, w_ref,
                o_ref, state_ref, *, n_group):
    @pl.when(pl.program_id(2) == 0)
    def _():
        state_ref[...] = jnp.zeros_like(state_ref)

    cos = cos_ref[...]
    sin = sin_ref[...]
    half = HEAD_DIM // 2
    lanes = [slice(g * HEAD_DIM, (g + 1) * HEAD_DIM) for g in range(n_group)]

    def rotate_and_score(g):
        q = q_ref[0, :, lanes[g]].astype(F32)
        k = k_ref[0, :, lanes[g]].astype(F32)
        qr = q * cos + pltpu.roll(q, half, 1) * sin
        kr = k * cos + pltpu.roll(k, half, 1) * sin
        qb = qr.astype(BF16)
        scores = lax.dot_general(qb, kr.astype(BF16), NT_DIMS, preferred_element_type=F32)
        carried = jnp.dot(qb, state_ref[g].astype(BF16), preferred_element_type=F32)
        return kr, scores, carried

    def mix_values(g, kr, scores, carried):
        v = v_ref[0, :, lanes[g]]
        ret = jnp.dot((scores * dec_ref[g]).astype(BF16), v, preferred_element_type=F32) + carried * xi_ref[g]
        kz = (kr * zeta_ref[g]).astype(BF16)
        state_ref[g] = state_ref[g] * gt_ref[g] + lax.dot_general(kz, v, TN_DIMS, preferred_element_type=F32)
        return ret

    def normalise(g, ret):
        mu = jnp.mean(ret, axis=-1, keepdims=True)
        cen = ret - mu
        var = jnp.mean(cen * cen, axis=-1, keepdims=True)
        gate = g_ref[0, :, lanes[g]].astype(F32)
        silu = gate / (1.0 + jnp.exp(-gate))
        o_ref[0, :, lanes[g]] = (cen * lax.rsqrt(var + NORM_EPS) * w_ref[:, lanes[g]] * silu).astype(o_ref.dtype)

    first, second = {}, {}
    for step in range(n_group + 2):
        if step < n_group:
            first[step] = rotate_and_score(step)
        if 0 <= step - 1 < n_group:
            second[step - 1] = mix_values(step - 1, *first.pop(step - 1))
        if 0 <= step - 2 < n_group:
            normalise(step - 2, second.pop(step - 2))


def _retention(proj, norm_w, n_heads, col0, n_group=4):
    b, s, _ = proj.shape
    t = 256
    d = HEAD_DIM
    inv_freq = ROPE_THETA ** (-jnp.arange(0, d, 2, dtype=F32) / d)
    ang = jnp.arange(s, dtype=F32)[:, None] * inv_freq[None, :]
    ang = jnp.concatenate([ang, ang], -1)
    sign = jnp.concatenate([-jnp.ones((d // 2,), F32), jnp.ones((d // 2,), F32)])
    cos_t = jnp.cos(ang)
    sin_t = jnp.sin(ang) * sign[None, :]
    log_g = jnp.log1p(-(2.0 ** (-5.0 - jnp.arange(n_heads, dtype=F32))))
    idx = jnp.arange(t, dtype=F32)
    same_or_earlier_chunk = (jnp.floor(idx[None, :] / CHUNK) <= jnp.floor(idx[:, None] / CHUNK))
    dec = jnp.exp(log_g[:, None, None] * jnp.abs(idx[:, None] - idx[None, :])) * (d ** -0.5)
    dec = jnp.where(same_or_earlier_chunk[None], dec, 0.0)
    xi = jnp.broadcast_to(jnp.exp(log_g[:, None] * (idx + 1.0))[:, :, None], (n_heads, t, d))
    zeta = jnp.broadcast_to((jnp.exp(log_g[:, None] * (t - 1.0 - idx)) * (d ** -0.5))[:, :, None], (n_heads, t, d))
    gt = jnp.broadcast_to(jnp.exp(log_g * t)[:, None, None], (n_heads, 1, d))
    width = n_group * d
    group_spec = lambda part: pl.BlockSpec(
        (1, t, width), lambda bi, hg, i: (bi, i, (col0 + part * n_heads) // n_group + hg))
    per_head = lambda rows, cols: pl.BlockSpec((n_group, rows, cols), lambda bi, hg, i: (hg, 0, 0))
    return pl.pallas_call(
        functools.partial(_ret_kernel, n_group=n_group),
        grid=(b, n_heads // n_group, s // t),
        in_specs=[group_spec(0), group_spec(1), group_spec(2), group_spec(3),
                  pl.BlockSpec((t, d), lambda bi, hg, i: (i, 0)),
                  pl.BlockSpec((t, d), lambda bi, hg, i: (i, 0)),
                  per_head(t, t), per_head(t, d), per_head(t, d), per_head(1, d),
                  pl.BlockSpec((1, width), lambda bi, hg, i: (0, hg))],
        out_specs=pl.BlockSpec((1, t, width), lambda bi, hg, i: (bi, i, hg)),
        out_shape=jax.ShapeDtypeStruct((b, s, n_heads * d), BF16),
        scratch_shapes=[pltpu.VMEM((n_group, d, d), F32)],
        compiler_params=_params(("parallel", "parallel", "arbitrary")),
        name="retention",
    )(proj, proj, proj, proj, cos_t, sin_t, dec, xi, zeta, gt, norm_w.reshape(1, -1).astype(F32))


def _layer_norm(y, w, b):
    mu = jnp.mean(y, axis=-1, keepdims=True)
    cen = y - mu
    var = jnp.mean(cen * cen, axis=-1, keepdims=True)
    return cen * lax.rsqrt(var + LN_EPS) * w + b


def _outproj_ln_kernel(sb_ref, ret_ref, wo_ref, x_ref, lw_ref, lb_ref, o_ref, ob_ref, *, alpha, split):
    mix = jnp.dot(sb_ref[...], wo_ref[:split, :], preferred_element_type=F32)
    mix = mix + jnp.dot(ret_ref[...], wo_ref[split:, :], preferred_element_type=F32)
    y = _layer_norm(alpha * x_ref[...] + mix, lw_ref[...], lb_ref[...])
    o_ref[...] = y
    ob_ref[...] = y.astype(BF16)


def _outproj_ln(sb, ret, wo, x, lw, lb, alpha):
    m, d = x.shape
    split = sb.shape[1]
    bm = 512
    kern = functools.partial(_outproj_ln_kernel, alpha=alpha, split=split)
    row = lambda width: pl.BlockSpec((bm, width), lambda i: (i, 0))
    full = lambda shape: pl.BlockSpec(shape, lambda i: (0, 0))
    return pl.pallas_call(
        kern,
        grid=(m // bm,),
        in_specs=[row(split), row(ret.shape[1]), full(wo.shape), row(d), full((1, d)), full((1, d))],
        out_specs=[row(d), row(d)],
        out_shape=[jax.ShapeDtypeStruct((m, d), F32), jax.ShapeDtypeStruct((m, d), BF16)],
        compiler_params=_params(("parallel",)),
        name="outproj_ln",
    )(sb, ret, wo, x, lw.reshape(1, d), lb.reshape(1, d))


def _top_ranks(s, iota, k, exact_ties):
    n = s.shape[0]
    work = s
    rank = jnp.full(s.shape, float(k), F32)
    tops = []
    for r in range(k):
        m = jnp.max(work, axis=0, keepdims=True)
        sel = work == m
        if exact_ties:
            first = jnp.min(jnp.where(sel, iota, float(n)), axis=0, keepdims=True)
            sel = iota == first
        rank = jnp.where(sel, float(r), rank)
        work = jnp.where(sel, -jnp.inf, work)
        tops.append(m)
    return rank, tops


def _route_column(s1, s2, iota, ciota, widths, exact_ties):
    k = PEER_TOPK
    nk, lanes = s1.shape
    n_cand = sum(widths)
    n_pad = ciota.shape[0]
    rank1, top1 = _top_ranks(s1, iota, k, exact_ties)
    rank2, top2 = _top_ranks(s2, iota, k, exact_ties)
    top2_all = jnp.concatenate(top2, axis=0)
    cand = [top1[a] + top2_all[:widths[a]] for a in range(k)]
    if n_pad > n_cand:
        cand.append(jnp.full((n_pad - n_cand, lanes), -jnp.inf, F32))
    cand = jnp.concatenate(cand, axis=0)
    crank, _ = _top_ranks(cand, ciota, k, exact_ties)
    chosen = crank < float(k)
    best = top1[0] + top2[0]
    z = jnp.sum(jnp.where(chosen, jnp.exp(cand - best), 0.0), axis=0, keepdims=True)
    ones = jnp.where(chosen, 1.0, 0.0)
    cnt = jnp.zeros((nk, lanes), F32)
    off = 0
    for a in range(k):
        cnt_a = jnp.sum(ones[off:off + widths[a]], axis=0, keepdims=True)
        cnt = jnp.where(rank1 == float(a), cnt_a, cnt)
        off += widths[a]
    picked = (jnp.sum(jnp.where(rank1 < float(k), 1.0, 0.0), axis=0, keepdims=True)
              + jnp.sum(jnp.where(rank2 < float(k), 1.0, 0.0), axis=0, keepdims=True)
              + jnp.sum(ones, axis=0, keepdims=True))
    unique = picked == float(3 * k)
    return (cnt, jnp.exp(s1 - top1[0]), rank2, jnp.exp(s2 - top2[0]) / z), unique


def _router_kernel(q_ref, keys_ref, first_ref, second_ref, *, lanes, cols_per_iter):
    nk = PEER_KEYS
    k = PEER_TOPK
    iota = lax.broadcasted_iota(jnp.int32, (nk, lanes), 0).astype(F32)
    widths = [k // (a + 1) for a in range(k)]
    n_pad = -(-sum(widths) // F32_SUBLANES) * F32_SUBLANES
    ciota = lax.broadcasted_iota(jnp.int32, (n_pad, lanes), 0).astype(F32)
    out_slots = ((first_ref, 0), (first_ref, 1), (second_ref, 0), (second_ref, 1))

    def columns(group, _):
        pending = []
        for j in range(cols_per_iter):
            c = group * cols_per_iter + j
            sl = pl.ds(pl.multiple_of(c * lanes, lanes), lanes)
            q = q_ref[sl, :]
            half = q.shape[1] // 2
            s1 = lax.dot_general(keys_ref[0, 0], q[:, :half], NT_DIMS, preferred_element_type=F32)
            s2 = lax.dot_general(keys_ref[0, 1], q[:, half:], NT_DIMS, preferred_element_type=F32)
            outs, unique = _route_column(s1, s2, iota, ciota, widths, exact_ties=False)
            for (ref, slot), val in zip(out_slots, outs):
                ref[slot, 0, c] = val
            pending.append((c, s1, s2, jnp.min(jnp.where(unique, 1.0, 0.0))))
        for c, s1, s2, all_unique in pending:
            @pl.when(all_unique < 0.5)
            def _():
                exact, _ = _route_column(s1, s2, iota, ciota, widths, exact_ties=True)
                for (ref, slot), val in zip(out_slots, exact):
                    ref[slot, 0, c] = val
        return 0

    lax.fori_loop(0, q_ref.shape[0] // (lanes * cols_per_iter), columns, 0)


def _router(qp, sub_keys, tb):
    t = qp.shape[0]
    h, _, nk, half = sub_keys.shape
    out_f32 = jax.ShapeDtypeStruct((2, h, t // LANES, nk, LANES), F32)
    out_spec = pl.BlockSpec((2, 1, tb // LANES, nk, LANES), lambda j, hh: (0, hh, j, 0, 0))
    return pl.pallas_call(
        functools.partial(_router_kernel, lanes=LANES, cols_per_iter=8),
        grid=(t // tb, h),
        in_specs=[pl.BlockSpec((tb, 2 * half), lambda j, hh: (j, hh)),
                  pl.BlockSpec((1, 2, nk, half), lambda j, hh: (hh, 0, 0, 0))],
        out_specs=[out_spec, out_spec],
        out_shape=[out_f32, out_f32],
        compiler_params=_params(("parallel", "parallel")),
        name="peer_router",
    )(qp, sub_keys)


def _row_to_packed_tile(ref, k, h, c, row, n_rows):
    rep = jnp.broadcast_to(ref[k, h, c, 0, row:row + 1, :], (BF16_SUBLANES, LANES))
    packed = rep.astype(BF16)
    return jnp.concatenate([packed] * (n_rows // BF16_SUBLANES), axis=0)


def _peer_kernel(u_ref, v_ref, first_ref, second_in, x_ref, ln_ref, o_ref,
                 act_ref, hid_ref, rank2_ref, e2_ref, xb_ref, *, n_sub, alpha):
    e = pl.program_id(1)
    tb = x_ref.shape[0]
    nk = PEER_KEYS
    n_col = tb // LANES

    @pl.when(e == 0)
    def _():
        xb_ref[...] = x_ref[...].astype(BF16)
        rank2_ref[...] = second_in[0].astype(BF16)
        e2_ref[...] = second_in[1].astype(BF16)
        o_ref[...] = jnp.zeros_like(o_ref)

    act_ref[...] = lax.dot_general(u_ref[...], xb_ref[...], NT_DIMS, preferred_element_type=F32)
    for ii in range(n_sub):
        for c in range(n_col):
            gate = jnp.zeros((nk, LANES), BF16)
            for h in range(PEER_HEADS):
                cnt = _row_to_packed_tile(first_ref, 0, h, c, ii, nk)
                e1 = _row_to_packed_tile(first_ref, 1, h, c, ii, nk)
                gate = gate + jnp.where(rank2_ref[h, c] < cnt, e2_ref[h, c] * e1, jnp.zeros((), BF16))
            a = act_ref[ii * nk:(ii + 1) * nk, c * LANES:(c + 1) * LANES]
            gelu = 0.5 * a * (1.0 + lax.erf(a * (2.0 ** -0.5)))
            hid_ref[c * LANES:(c + 1) * LANES, ii * nk:(ii + 1) * nk] = (gelu.astype(BF16) * gate).T
    o_ref[...] += jnp.dot(hid_ref[...], v_ref[...], preferred_element_type=F32)

    @pl.when(e == pl.num_programs(1) - 1)
    def _():
        o_ref[...] = _layer_norm(alpha * x_ref[...] + o_ref[...], ln_ref[0:1, :], ln_ref[1:2, :])


def _peer_ffn_ln(x, u, v, first, second, lw, lb, alpha, tb, eb):
    t, d = x.shape
    n_exp = u.shape[0]
    _, h, n_col_all, nk, _ = first.shape
    assert eb == F32_SUBLANES * nk
    tokens = pl.BlockSpec((tb, d), lambda j, e: (j, 0))
    table = pl.BlockSpec((eb, d), lambda j, e: (e, 0))
    route = pl.BlockSpec((2, h, tb // LANES, nk, LANES), lambda j, e: (0, 0, j, 0, 0))
    first_key_rows = pl.BlockSpec((2, h, tb // LANES, 1, F32_SUBLANES, LANES), lambda j, e: (0, 0, j, e, 0, 0))
    ln_params = pl.BlockSpec((2, d), lambda j, e: (0, 0))
    first = first.reshape(2, h, n_col_all, nk // F32_SUBLANES, F32_SUBLANES, LANES)
    packed = pltpu.VMEM((h, tb // LANES, nk, LANES), BF16)
    return pl.pallas_call(
        functools.partial(_peer_kernel, n_sub=eb // nk, alpha=alpha),
        grid=(t // tb, n_exp // eb),
        in_specs=[table, table, first_key_rows, route, tokens, ln_params],
        out_specs=tokens,
        out_shape=jax.ShapeDtypeStruct((t, d), F32),
        scratch_shapes=[pltpu.VMEM((eb, tb), F32), pltpu.VMEM((tb, eb), BF16), packed, packed,
                        pltpu.VMEM((tb, d), BF16)],
        compiler_params=_params(("parallel", "arbitrary")),
        name="peer_ffn_ln",
    )(u, v, first, second, x, jnp.stack([lw, lb]))


def _layer(x, w_in, sb_norm_w, ret_norm_w, w_out, ln1_w, ln1_b, w_pq, sub_keys, expert_u, expert_v,
           ln2_w, ln2_b, alpha):
    b, s, d = x.shape
    n_tok = b * s
    sb_heads = sb_norm_w.shape[0] // HEAD_DIM
    ret_heads = ret_norm_w.shape[0] // HEAD_DIM
    xt = x.reshape(n_tok, d)
    col_scale = jnp.where(jnp.arange(w_in.shape[1]) < sb_heads * HEAD_DIM, SB_Q_SCALE, 1.0).astype(F32)
    proj = _matmul(xt, w_in, 1024, 1024, BF16, col_scale).reshape(b, s, -1)
    sb = _sb_attention(proj, sb_norm_w, sb_heads, 0)
    ret = _retention(proj, ret_norm_w, ret_heads, 3 * sb_heads)
    x1, x1b = _outproj_ln(sb.reshape(n_tok, -1), ret.reshape(n_tok, -1), w_out.astype(BF16), xt,
                          ln1_w, ln1_b, alpha)
    qp = _matmul(x1b, w_pq, 1024, 2048, BF16)
    first, second = _router(qp, sub_keys.astype(BF16), 1024)
    out = _peer_ffn_ln(x1, expert_u.astype(BF16), expert_v.astype(BF16), first, second, ln2_w, ln2_b,
                       alpha, 512, 1024)
    return out.reshape(b, s, d)


def kernel(x, w_in, sb_norm_w, ret_norm_w, w_out, ln1_w, ln1_b, w_pq, sub_keys, expert_u, expert_v, ln2_w, ln2_b):
    depth = w_in.shape[0]
    alpha = (2.0 * depth) ** 0.25
    for layer in range(depth):
        x = _layer(x, w_in[layer], sb_norm_w[layer], ret_norm_w[layer], w_out[layer], ln1_w[layer],
                   ln1_b[layer], w_pq[layer], sub_keys[layer], expert_u[layer], expert_v[layer],
                   ln2_w[layer], ln2_b[layer], alpha)
    return x
```
